```python
import jax, jax.numpy as jnp
from jax import lax
import numpy as np

D_MODEL = 1024
BATCH = 8
SEQ = 2048
DEPTH = 4
DEC_BATCH = 128
DEC_SEQ = 8
PAST_LEN = 16384
PAGE_SIZE = 128

N_META = 16
N_HEADS = 8
HEAD_K = 128
HEAD_V = D_MODEL // N_HEADS
KEY_DIM = N_HEADS * HEAD_K
CHUNK = 16
CONV_W = 3
D_FF = -(-8 * D_MODEL // (3 * 256)) * 256
N_HGRN = (DEPTH + 1) // 2
N_CONV = DEPTH // 2
EPS = 1e-6

kernel_name = "hgrn2_shortconv_hybrid_step"


def rmsnorm(x, g):
    xf = x.astype(jnp.float32)
    y = xf * lax.rsqrt(jnp.mean(xf * xf, axis=-1, keepdims=True) + EPS)
    return (y * g.astype(jnp.float32)).astype(x.dtype)


def swiglu(h, w_gate, w_up, w_down):
    return (jax.nn.silu(h @ w_gate) * (h @ w_up)) @ w_down


def layer_lower_bounds(lb_logits):
    sm = jax.nn.softmax(lb_logits.astype(jnp.float32), axis=0)
    return jnp.cumsum(sm, axis=0) - sm[0:1]


def _chunk_major(a, n):
    b, t, h, e = a.shape
    return a.reshape(b, n, CHUNK, h, e).transpose(1, 0, 3, 2, 4)


def gla_chunked(q, k, v, log_f, s0):
    bsz, t = q.shape[:2]
    pad = (-t) % CHUNK
    pw = ((0, 0), (0, pad), (0, 0), (0, 0))
    q, k, v, log_f = [jnp.pad(a, pw) for a in (q, k, v, log_f)]
    n = (t + pad) // CHUNK
    xs = tuple(_chunk_major(a, n) for a in (q, k, v, log_f))
    causal = jnp.tril(jnp.ones((CHUNK, CHUNK), dtype=bool))[:, :, None]

    def step(S, inp):
        qc, kc, vc, gc = inp
        b = jnp.cumsum(gc, axis=2)
        diff = b[:, :, :, None, :] - b[:, :, None, :, :]
        decay = jnp.where(causal, jnp.exp(jnp.where(causal, diff, 0.0)), 0.0)
        scores = jnp.einsum('bhid,bhjd,bhijd->bhij', qc, kc, decay)
        o = (jnp.einsum('bhij,bhje->bhie', scores, vc)
             + jnp.einsum('bhid,bhde->bhie', qc * jnp.exp(b), S))
        b_last = b[:, :, -1:, :]
        S = (jnp.exp(b_last[:, :, 0, :, None]) * S
             + jnp.einsum('bhjd,bhje->bhde', kc * jnp.exp(b_last - b), vc))
        return S, o

    S, o = lax.scan(step, s0, xs)
    o = o.transpose(1, 0, 3, 2, 4).reshape(bsz, n * CHUNK, N_HEADS, -1)[:, :t]
    return o, S


def hgrn2_mixer(h, s0, w_in, w_out, lb, g_norm):
    bsz, t, _ = h.shape
    proj = h @ w_in
    q = proj[..., :KEY_DIM]
    z = proj[..., KEY_DIM:2 * KEY_DIM]
    i = proj[..., 2 * KEY_DIM:2 * KEY_DIM + D_MODEL]
    g = proj[..., 2 * KEY_DIM + D_MODEL:]
    q = jax.nn.silu(q.astype(jnp.float32)).reshape(bsz, t, N_HEADS, HEAD_K)
    z = z.astype(jnp.float32).reshape(bsz, t, N_HEADS, HEAD_K)
    lbh = lb.reshape(N_HEADS, HEAD_K)
    log_f = jnp.log(lbh + (1.0 - lbh) * jax.nn.sigmoid(z))
    k = (1.0 - lbh) * jax.nn.sigmoid(-z)
    v = i.astype(jnp.float32).reshape(bsz, t, N_HEADS, HEAD_V)
    o, s = gla_chunked(q, k, v, log_f, s0.astype(jnp.float32))
    o = rmsnorm(o.reshape(bsz, t, D_MODEL), g_norm) * jax.nn.silu(g.astype(jnp.float32))
    return o.astype(h.dtype) @ w_out, s


def shortconv_mixer(h, buf, w_in, w_conv, w_out):
    t = h.shape[1]
    proj = h @ w_in
    gate_b = proj[..., :D_MODEL]
    gate_c = proj[..., D_MODEL:2 * D_MODEL]
    u = gate_c * proj[..., 2 * D_MODEL:]
    full = jnp.concatenate([buf.astype(u.dtype), u], axis=1)
    y = sum(w_conv[tap] * full[:, tap:tap + t] for tap in range(CONV_W))
    return (gate_b * y) @ w_out, full[:, -(CONV_W - 1):]


def trunk(x, s_hgrn, s_conv, lb, norm_mix, norm_ffn, norm_final, hgrn_w_in, hgrn_w_out,
          hgrn_norm, conv_w_in, conv_w, conv_w_out, ffn_w_gate, ffn_w_up, ffn_w_down):
    new_h, new_c = [], []
    for l in range(DEPTH):
        h = rmsnorm(x, norm_mix[l])
        j = l // 2
        if l % 2 == 0:
            m, s = hgrn2_mixer(h, s_hgrn[j], hgrn_w_in[j], hgrn_w_out[j], lb[j], hgrn_norm[j])
            new_h.append(s)
        else:
            m, s = shortconv_mixer(h, s_conv[j], conv_w_in[j], conv_w[j], conv_w_out[j])
            new_c.append(s)
        x = x + m.astype(x.dtype)
        x = x + swiglu(rmsnorm(x, norm_ffn[l]), ffn_w_gate[l], ffn_w_up[l], ffn_w_down[l]).astype(x.dtype)
    return rmsnorm(x, norm_final), jnp.stack(new_h), jnp.stack(new_c)


def setup_inputs(seed: int = 0) -> dict:
    key = jax.random.key(seed)
    ks = jax.random.split(key, 20)

    def nrm(k, shape, scale):
        return jax.random.normal(k, shape, jnp.float32) * scale

    return {
        "x_prompt": nrm(ks[0], (BATCH, SEQ, D_MODEL), 1.0),
        "x_sample": nrm(ks[1], (DEC_BATCH, DEC_SEQ, D_MODEL), 1.0),
        "state_hgrn": nrm(ks[2], (N_HGRN, DEC_BATCH, N_HEADS, HEAD_K, HEAD_V), 0.5),
        "state_conv": nrm(ks[3], (N_CONV, DEC_BATCH, CONV_W - 1, D_MODEL), 1.0),
        "meta_tokens": nrm(ks[4], (N_META, D_MODEL), 1.0),
        "norm_mix": 1.0 + nrm(ks[5], (DEPTH, D_MODEL), 0.01),
        "norm_ffn": 1.0 + nrm(ks[6], (DEPTH, D_MODEL), 0.01),
        "norm_final": 1.0 + nrm(ks[7], (D_MODEL,), 0.01),
        "hgrn_w_in": nrm(ks[8], (N_HGRN, D_MODEL, 2 * KEY_DIM + 2 * D_MODEL), D_MODEL ** -0.5),
        "hgrn_w_out": nrm(ks[9], (N_HGRN, D_MODEL, D_MODEL), D_MODEL ** -0.5),
        "hgrn_lb_logits": nrm(ks[10], (N_HGRN, KEY_DIM), 0.5),
        "hgrn_norm": 1.0 + nrm(ks[11], (N_HGRN, D_MODEL), 0.01),
        "conv_w_in": nrm(ks[12], (N_CONV, D_MODEL, 3 * D_MODEL), D_MODEL ** -0.5),
        "conv_w": nrm(ks[13], (N_CONV, CONV_W, D_MODEL), CONV_W ** -0.5),
        "conv_w_out": nrm(ks[14], (N_CONV, D_MODEL, D_MODEL), D_MODEL ** -0.5),
        "ffn_w_gate": nrm(ks[15], (DEPTH, D_MODEL, D_FF), D_MODEL ** -0.5),
        "ffn_w_up": nrm(ks[16], (DEPTH, D_MODEL, D_FF), D_MODEL ** -0.5),
        "ffn_w_down": nrm(ks[17], (DEPTH, D_FF, D_MODEL), D_FF ** -0.5),
    }


def reference(x_prompt, x_sample, state_hgrn, state_conv, meta_tokens, norm_mix, norm_ffn,
              norm_final, hgrn_w_in, hgrn_w_out, hgrn_lb_logits, hgrn_norm, conv_w_in, conv_w,
              conv_w_out, ffn_w_gate, ffn_w_up, ffn_w_down):
    lb = layer_lower_bounds(hgrn_lb_logits)
    weights = (norm_mix, norm_ffn, norm_final, hgrn_w_in, hgrn_w_out, hgrn_norm,
               conv_w_in, conv_w, conv_w_out, ffn_w_gate, ffn_w_up, ffn_w_down)
    bsz = x_prompt.shape[0]
    meta = jnp.broadcast_to(meta_tokens.astype(x_prompt.dtype)[None], (bsz, N_META, D_MODEL))
    xp = jnp.concatenate([meta, x_prompt], axis=1)
    zeros_h = jnp.zeros((N_HGRN, bsz, N_HEADS, HEAD_K, HEAD_V), jnp.float32)
    zeros_c = jnp.zeros((N_CONV, bsz, CONV_W - 1, D_MODEL), x_prompt.dtype)
    yp, new_hgrn_prompt, new_conv_prompt = trunk(xp, zeros_h, zeros_c, lb, *weights)
    y_sample, new_hgrn_sample, new_conv_sample = trunk(x_sample, state_hgrn, state_conv, lb, *weights)
    return (yp[:, N_META:], y_sample, new_hgrn_prompt, new_conv_prompt, new_hgrn_sample, new_conv_sample)
```

```python
import functools

import jax
import jax.numpy as jnp
from jax import lax
from jax.experimental import pallas as pl
from jax.experimental.pallas import tpu as pltpu

F32, BF16 = jnp.float32, jnp.bfloat16
EPS = 1e-6
CONV_W = 3

V7X_VMEM_BYTES = 64 * 1024 * 1024
V7X_VMEM_COMPILER_RESERVE = 6 * 1024 * 1024
BF16_SUBLANES = 16

ROW_TILE = 512
GLA_CHUNK = 64
GLA_SEQS_PER_STEP = 8
CONV_ROWS = 512
FF_CHUNK = 1408


def _vmem_limit(est_bytes):
    return int(min(max(est_bytes, 16 * 1024 * 1024), V7X_VMEM_BYTES - V7X_VMEM_COMPILER_RESERVE))


def _rmsnorm(x, g):
    return x * lax.rsqrt(jnp.mean(x * x, axis=-1, keepdims=True) + EPS) * g


def _silu(x):
    return x * jax.nn.sigmoid(x)


def _dot(a, b):
    return jnp.dot(a, b, preferred_element_type=F32)


def _dot_nt(a, b):
    return lax.dot_general(a, b, (((1,), (1,)), ((), ())), preferred_element_type=F32)


def _dot_tn(a, b):
    return lax.dot_general(a, b, (((0,), (0,)), ((), ())), preferred_element_type=F32)


def _resident(shape):
    zeros = (0,) * len(shape)
    return pl.BlockSpec(shape, lambda *_: zeros, pipeline_mode=pl.Buffered(1))


def _norm_matmul_body(x_ref, g_ref, w_ref, o_ref, *, n_chunk):
    h = _rmsnorm(x_ref[...], g_ref[...]).astype(BF16)
    for c in range(0, o_ref.shape[1], n_chunk):
        o_ref[:, c:c + n_chunk] = _dot(h, w_ref[:, c:c + n_chunk])


def _norm_matmul(x, g, w, *, tm):
    r, d = x.shape
    n = w.shape[1]
    est = 2 * tm * d * 4 + d * n * 2 + 2 * tm * n * 4 + tm * d * 8 + tm * 1024 * 8
    return pl.pallas_call(
        functools.partial(_norm_matmul_body, n_chunk=1024),
        grid=(r // tm,),
        in_specs=[pl.BlockSpec((tm, d), lambda i: (i, 0)), _resident((1, d)), _resident((d, n))],
        out_specs=pl.BlockSpec((tm, n), lambda i: (i, 0)),
        out_shape=jax.ShapeDtypeStruct((r, n), F32),
        compiler_params=pltpu.CompilerParams(
            dimension_semantics=("parallel",), vmem_limit_bytes=_vmem_limit(est)),
        name="hgrn_in_proj",
    )(x, g, w)


def _gla_body(proj_ref, s0_ref, lbl_ref, gn_ref, og_ref, snew_ref, st_ref, o_ref, *,
              layer, nb, t_rows, bt, n_heads, hk, hv):
    c = pl.program_id(1)
    key = n_heads * hk
    d = n_heads * hv
    pad = bt - t_rows

    def for_each_seq(fn):
        if nb == 1:
            fn(0)
        else:
            lax.fori_loop(0, nb, lambda i, _: (fn(i), 0)[1], 0)

    @pl.when(c == 0)
    def _():
        def load(i):
            for h in range(n_heads):
                st_ref[i, h] = s0_ref[i, h].T
        for_each_seq(load)

    logits = lbl_ref[...]
    ex = jnp.exp(logits - jnp.max(logits, axis=0, keepdims=True))
    sm = ex / jnp.sum(ex, axis=0, keepdims=True)
    lb = jnp.sum(sm[0:layer + 1], axis=0, keepdims=True) - sm[0:1]

    row = lax.broadcasted_iota(jnp.int32, (bt, hk), 0)
    ri = lax.broadcasted_iota(jnp.int32, (bt, bt), 0)
    ci = lax.broadcasted_iota(jnp.int32, (bt, bt), 1)
    xr = ri ^ ci
    levels = [1 << p for p in range(bt.bit_length() - 1)]
    level_mask = {s: (ri > ci) & (xr >= s) & (xr < 2 * s) for s in levels}

    def padded(x):
        return x if pad == 0 else jnp.concatenate([x, jnp.zeros((pad, x.shape[1]), x.dtype)], axis=0)

    def one_seq(i):
        r0 = i * t_rows if nb == 1 else pl.multiple_of(i * t_rows, t_rows)
        rows = pl.ds(r0, t_rows)
        for h in range(n_heads):
            q = padded(proj_ref[rows, h * hk:(h + 1) * hk])
            z = padded(proj_ref[rows, key + h * hk:key + (h + 1) * hk])
            v = padded(proj_ref[rows, 2 * key + h * hv:2 * key + (h + 1) * hv])
            lbh = lb[:, h * hk:(h + 1) * hk]
            om = 1.0 - lbh
            qs = _silu(q)
            sig = jax.nn.sigmoid(z)
            lf = jnp.log(lbh + om * sig)
            k = om * (1.0 - sig)
            if pad:
                lf = jnp.where(row < t_rows, lf, 0.0)
                k = jnp.where(row < t_rows, k, 0.0)
            b = lf
            for s in levels:
                b = b + jnp.where(row >= s, pltpu.roll(b, s, axis=0), 0.0)
            vb = v.astype(BF16)

            a = jnp.where(ri == ci, _dot_nt(qs.astype(BF16), k.astype(BF16)), 0.0)
            e = b
            for s in levels:
                second = (row & s) != 0
                r = jnp.where(second, pltpu.roll(e, s, axis=0), e)
                if 2 * s < bt:
                    e = jnp.where(second, e, pltpu.roll(e, bt - s, axis=0))
                x = (jnp.where(second, qs, k) * jnp.exp(-jnp.abs(b - r))).astype(BF16)
                a = jnp.where(level_mask[s], _dot_nt(x, x), a)

            st = st_ref[i, h]
            qb = (qs * jnp.exp(b)).astype(BF16)
            o = _dot_nt(qb, st.astype(BF16)) + _dot(a.astype(BF16), vb)
            o_ref[rows, h * hv:(h + 1) * hv] = o[0:t_rows]
            bl = b[bt - 1:bt, :]
            kt = (k * jnp.exp(bl - b)).astype(BF16)
            st_ref[i, h] = st * jnp.exp(bl) + _dot_tn(vb, kt)

    for_each_seq(one_seq)

    gate = proj_ref[:, 2 * key + d:2 * key + 2 * d]
    og_ref[...] = (_rmsnorm(o_ref[...], gn_ref[...]) * _silu(gate)).astype(BF16)

    @pl.when(c == pl.num_programs(1) - 1)
    def _():
        def store(i):
            for h in range(n_heads):
                snew_ref[i, h] = st_ref[i, h].T
        for_each_seq(store)


def _gla(proj, s0, lb_logits, gn, *, layer, n_seq, t):
    _, n_heads, hk, hv = s0.shape
    d = n_heads * hv
    if t >= GLA_CHUNK:
        nb, t_rows = 1, GLA_CHUNK
    else:
        nb, t_rows = min(GLA_SEQS_PER_STEP, n_seq), t
    bt = max(t_rows, BF16_SUBLANES)
    nc = t // t_rows
    rows = nb * t_rows
    assert t % t_rows == 0 and n_seq % nb == 0 and (nb == 1 or nc == 1) and rows % BF16_SUBLANES == 0
    state_bytes = nb * n_heads * hk * hv * 4
    est = 2 * rows * proj.shape[1] * 4 + 5 * state_bytes + 4 * rows * d * 4 + 8 * 1024 * 1024
    body = functools.partial(_gla_body, layer=layer, nb=nb, t_rows=t_rows, bt=bt,
                             n_heads=n_heads, hk=hk, hv=hv)
    state_spec = pl.BlockSpec((nb, n_heads, hk, hv), lambda b, c: (b, 0, 0, 0))
    return pl.pallas_call(
        body,
        grid=(n_seq // nb, nc),
        in_specs=[pl.BlockSpec((rows, proj.shape[1]), lambda b, c: (b * nc + c, 0)),
                  state_spec, _resident(lb_logits.shape), _resident((1, d))],
        out_specs=[pl.BlockSpec((rows, d), lambda b, c: (b * nc + c, 0)), state_spec],
        out_shape=[jax.ShapeDtypeStruct((n_seq * t, d), BF16), jax.ShapeDtypeStruct(s0.shape, F32)],
        scratch_shapes=[pltpu.VMEM((nb, n_heads, hv, hk), F32), pltpu.VMEM((rows, d), F32)],
        compiler_params=pltpu.CompilerParams(
            dimension_semantics=("parallel", "arbitrary"), vmem_limit_bytes=_vmem_limit(est)),
        name="hgrn_gla",
    )(proj, s0, lb_logits, gn)


def _conv_body(x_ref, g_ref, w_ref, wc_ref, buf_ref, a_ref, nbuf_ref, *, nb, tt):
    c = pl.program_id(1)
    d = x_ref.shape[1]

    @pl.when(c == 0)
    def _():
        nbuf_ref[...] = buf_ref[...]

    h = _rmsnorm(x_ref[...], g_ref[...]).astype(BF16)
    gate_b = _dot(h, w_ref[:, 0:d])
    u = (_dot(h, w_ref[:, d:2 * d]) * _dot(h, w_ref[:, 2 * d:3 * d])).reshape(nb, tt, d)
    prev = nbuf_ref[...]
    p2, p1 = prev[:, 0:1, :], prev[:, 1:2, :]
    t = lax.broadcasted_iota(jnp.int32, (nb, tt, d), 1)
    u1 = jnp.where(t == 0, p1, pltpu.roll(u, 1, axis=1))
    u2 = jnp.where(t == 0, p2, jnp.where(t == 1, p1, pltpu.roll(u, 2, axis=1)))
    w0 = wc_ref[0:1, :].reshape(1, 1, d)
    w1 = wc_ref[1:2, :].reshape(1, 1, d)
    w2 = wc_ref[2:3, :].reshape(1, 1, d)
    y = w0 * u2 + w1 * u1 + w2 * u
    a_ref[...] = (gate_b * y.reshape(nb * tt, d)).astype(BF16)
    nbuf_ref[...] = u[:, tt - 2:tt, :]


def _conv_mixer(x, g, w_in, w_conv, buf, *, n_seq, t):
    r, d = x.shape
    if t >= CONV_ROWS:
        nb, tt = 1, CONV_ROWS
    else:
        nb, tt = min(CONV_ROWS // t, n_seq), t
    nc = t // tt
    rows = nb * tt
    assert t % tt == 0 and n_seq % nb == 0 and (nb == 1 or nc == 1) and tt % 8 == 0 and tt >= CONV_W - 1
    est = 2 * rows * d * 4 + d * 3 * d * 2 + 2 * rows * d * 2 + 10 * rows * d * 4
    buf_spec = pl.BlockSpec((nb, CONV_W - 1, d), lambda b, c: (b, 0, 0))
    return pl.pallas_call(
        functools.partial(_conv_body, nb=nb, tt=tt),
        grid=(n_seq // nb, nc),
        in_specs=[pl.BlockSpec((rows, d), lambda b, c: (b * nc + c, 0)), _resident((1, d)),
                  _resident(w_in.shape), _resident(w_conv.shape), buf_spec],
        out_specs=[pl.BlockSpec((rows, d), lambda b, c: (b * nc + c, 0)), buf_spec],
        out_shape=[jax.ShapeDtypeStruct((r, d), BF16), jax.ShapeDtypeStruct(buf.shape, F32)],
        compiler_params=pltpu.CompilerParams(
            dimension_semantics=("parallel", "arbitrary"), vmem_limit_bytes=_vmem_limit(est)),
        name="conv_mixer",
    )(x, g, w_in, w_conv, buf)


def _out_ffn_body(x_ref, a_ref, wo_ref, g_ref, wg_ref, wu_ref, wd_ref, gf_ref, y_ref, *,
                  ff_chunk, final):
    x = x_ref[...] + _dot(a_ref[...], wo_ref[...])
    h = _rmsnorm(x, g_ref[...]).astype(BF16)
    for c in range(0, wg_ref.shape[1], ff_chunk):
        act = (_silu(_dot(h, wg_ref[:, c:c + ff_chunk])) * _dot(h, wu_ref[:, c:c + ff_chunk]))
        x = x + _dot(act.astype(BF16), wd_ref[c:c + ff_chunk, :])
    y_ref[...] = _rmsnorm(x, gf_ref[...]) if final else x


def _out_ffn(x, a, w_out, g, w_gate, w_up, w_down, g_final, *, tm, final):
    r, d = x.shape
    dff = w_gate.shape[1]
    assert dff % FF_CHUNK == 0
    weights = (d * d + 3 * d * dff) * 2
    est = 2 * tm * d * (4 + 2 + 4) + weights + 3 * tm * d * 4 + tm * FF_CHUNK * 14
    row = lambda i: (i, 0)
    return pl.pallas_call(
        functools.partial(_out_ffn_body, ff_chunk=FF_CHUNK, final=final),
        grid=(r // tm,),
        in_specs=[pl.BlockSpec((tm, d), row), pl.BlockSpec((tm, d), row), _resident(w_out.shape),
                  _resident((1, d)), _resident(w_gate.shape), _resident(w_up.shape),
                  _resident(w_down.shape), _resident((1, d))],
        out_specs=pl.BlockSpec((tm, d), row),
        out_shape=jax.ShapeDtypeStruct((r, d), F32),
        compiler_params=pltpu.CompilerParams(
            dimension_semantics=("parallel",), vmem_limit_bytes=_vmem_limit(est)),
        name="out_ffn",
    )(x, a, w_out, g, w_gate, w_up, w_down, g_final)


def _trunk(x, s_hgrn, s_conv, p, *, n_seq, t):
    depth = p["norm_mix"].shape[0]
    tm = min(ROW_TILE, x.shape[0])
    assert x.shape[0] % tm == 0
    new_h, new_c = [], []
    for l in range(depth):
        j = l // 2
        g_mix = p["norm_mix"][l][None]
        if l % 2 == 0:
            proj = _norm_matmul(x, g_mix, p["hgrn_w_in"][j], tm=tm)
            a, s = _gla(proj, s_hgrn[j], p["hgrn_lb_logits"], p["hgrn_norm"][j][None],
                        layer=j, n_seq=n_seq, t=t)
            new_h.append(s)
            w_out = p["hgrn_w_out"][j]
        else:
            a, s = _conv_mixer(x, g_mix, p["conv_w_in"][j], p["conv_w"][j], s_conv[j], n_seq=n_seq, t=t)
            new_c.append(s)
            w_out = p["conv_w_out"][j]
        x = _out_ffn(x, a, w_out, p["norm_ffn"][l][None], p["ffn_w_gate"][l], p["ffn_w_up"][l],
                     p["ffn_w_down"][l], p["norm_final"][None], tm=tm, final=(l == depth - 1))
    return x, jnp.stack(new_h), jnp.stack(new_c)


def kernel(x_prompt, x_sample, state_hgrn, state_conv, meta_tokens, norm_mix, norm_ffn, norm_final,
           hgrn_w_in, hgrn_w_out, hgrn_lb_logits, hgrn_norm, conv_w_in, conv_w, conv_w_out,
           ffn_w_gate, ffn_w_up, ffn_w_down):
    bsz, seq, d = x_prompt.shape
    dec_b, dec_t, _ = x_sample.shape
    n_meta = meta_tokens.shape[0]
    p = dict(norm_mix=norm_mix, norm_ffn=norm_ffn, norm_final=norm_final, hgrn_lb_logits=hgrn_lb_logits,
             hgrn_norm=hgrn_norm, conv_w=conv_w,
             hgrn_w_in=hgrn_w_in.astype(BF16), hgrn_w_out=hgrn_w_out.astype(BF16),
             conv_w_in=conv_w_in.astype(BF16), conv_w_out=conv_w_out.astype(BF16),
             ffn_w_gate=ffn_w_gate.astype(BF16), ffn_w_up=ffn_w_up.astype(BF16),
             ffn_w_down=ffn_w_down.astype(BF16))

    zeros_h = jnp.zeros((state_hgrn.shape[0], 1) + state_hgrn.shape[2:], F32)
    zeros_c = jnp.zeros((state_conv.shape[0], 1) + state_conv.shape[2:], F32)
    _, meta_h, meta_c = _trunk(meta_tokens.astype(F32), zeros_h, zeros_c, p, n_seq=1, t=n_meta)

    prompt_h0 = jnp.broadcast_to(meta_h, (meta_h.shape[0], bsz) + meta_h.shape[2:])
    prompt_c0 = jnp.broadcast_to(meta_c, (meta_c.shape[0], bsz) + meta_c.shape[2:])
    y_prompt, new_hgrn_prompt, new_conv_prompt = _trunk(
        x_prompt.reshape(bsz * seq, d), prompt_h0, prompt_c0, p, n_seq=bsz, t=seq)

    y_sample, new_hgrn_sample, new_conv_sample = _trunk(
        x_sample.reshape(dec_b * dec_t, d), state_hgrn, state_conv, p, n_seq=dec_b, t=dec_t)

    return (y_prompt.reshape(bsz, seq, d), y_sample.reshape(dec_b, dec_t, d),
            new_hgrn_prompt, new_conv_prompt, new_hgrn_sample, new_conv_sample)
```

```python
import functools

import jax
import jax.numpy as jnp
from jax import lax
from jax.experimental import pallas as pl
from jax.experimental.pallas import tpu as pltpu

F32, BF16 = jnp.float32, jnp.bfloat16
EPS = 1e-6
CONV_W = 3

V7X_VMEM_BYTES = 64 * 1024 * 1024
V7X_VMEM_COMPILER_RESERVE = 6 * 1024 * 1024
BF16_SUBLANES = 16

ROW_TILE = 512
GLA_CHUNK = 64
GLA_SEQS_PER_STEP = 8
CONV_ROWS = 512
FF_CHUNK = 1408


def _vmem_limit(est_bytes):
    return int(min(max(est_bytes, 16 * 1024 * 1024), V7X_VMEM_BYTES - V7X_VMEM_COMPILER_RESERVE))


def _rmsnorm(x, g):
    return x * lax.rsqrt(jnp.mean(x * x, axis=-1, keepdims=True) + EPS) * g


def _silu(x):
    return x * jax.nn.sigmoid(x)


def _dot(a, b):
    return jnp.dot(a, b, preferred_element_type=F32)


def _dot_nt(a, b):
    return lax.dot_general(a, b, (((1,), (1,)), ((), ())), preferred_element_type=F32)


def _dot_tn(a, b):
    return lax.dot_general(a, b, (((0,), (0,)), ((), ())), preferred_element_type=F32)


def _resident(shape):
    zeros = (0,) * len(shape)
    return pl.BlockSpec(shape, lambda *_: zeros, pipeline_mode=pl.Buffered(1))


def _layer_of(stacked, l):
    index = (l,) + (0,) * (stacked.ndim - 1)
    return pl.BlockSpec((None,) + stacked.shape[1:], lambda *_: index, pipeline_mode=pl.Buffered(1))


def _norm_matmul_body(x_ref, g_ref, w_ref, o_ref, *, n_chunk):
    h = _rmsnorm(x_ref[...], g_ref[...]).astype(BF16)
    for c in range(0, o_ref.shape[1], n_chunk):
        o_ref[:, c:c + n_chunk] = _dot(h, w_ref[:, c:c + n_chunk])


def _norm_matmul(x, g, w, *, l, j, tm):
    r, d = x.shape
    n = w.shape[2]
    est = 2 * tm * d * 4 + d * n * 2 + 2 * tm * n * 4 + tm * d * 8 + tm * 1024 * 8
    return pl.pallas_call(
        functools.partial(_norm_matmul_body, n_chunk=1024),
        grid=(r // tm,),
        in_specs=[pl.BlockSpec((tm, d), lambda i: (i, 0)), _layer_of(g, l), _layer_of(w, j)],
        out_specs=pl.BlockSpec((tm, n), lambda i: (i, 0)),
        out_shape=jax.ShapeDtypeStruct((r, n), F32),
        compiler_params=pltpu.CompilerParams(
            dimension_semantics=("parallel",), vmem_limit_bytes=_vmem_limit(est)),
        name="hgrn_in_proj",
    )(x, g, w)


def _gla_body(proj_ref, s0_ref, lbl_ref, gn_ref, *rest, layer, nb, t_rows, bt, n_heads, hk, hv):
    og_ref, snew_ref, st_ref, o_ref = rest[-4:]
    c = pl.program_id(1)
    key = n_heads * hk
    d = n_heads * hv
    pad = bt - t_rows

    def for_each_seq(fn):
        if nb == 1:
            fn(0)
        else:
            lax.fori_loop(0, nb, lambda i, _: (fn(i), 0)[1], 0)

    @pl.when(c == 0)
    def _():
        def load(i):
            for h in range(n_heads):
                st_ref[i, h] = s0_ref[i, h].T
        for_each_seq(load)

    logits = lbl_ref[...]
    ex = jnp.exp(logits - jnp.max(logits, axis=0, keepdims=True))
    sm = ex / jnp.sum(ex, axis=0, keepdims=True)
    lb = jnp.sum(sm[0:layer + 1], axis=0, keepdims=True) - sm[0:1]

    row = lax.broadcasted_iota(jnp.int32, (bt, hk), 0)
    ri = lax.broadcasted_iota(jnp.int32, (bt, bt), 0)
    ci = lax.broadcasted_iota(jnp.int32, (bt, bt), 1)
    xr = ri ^ ci
    levels = [1 << p for p in range(bt.bit_length() - 1)]
    level_mask = {s: (ri > ci) & (xr >= s) & (xr < 2 * s) for s in levels}

    def padded(x):
        return x if pad == 0 else jnp.concatenate([x, jnp.zeros((pad, x.shape[1]), x.dtype)], axis=0)

    def one_seq(i):
        r0 = i * t_rows if nb == 1 else pl.multiple_of(i * t_rows, t_rows)
        rows = pl.ds(r0, t_rows)
        for h in range(n_heads):
            q = padded(proj_ref[rows, h * hk:(h + 1) * hk])
            z = padded(proj_ref[rows, key + h * hk:key + (h + 1) * hk])
            v = padded(proj_ref[rows, 2 * key + h * hv:2 * key + (h + 1) * hv])
            lbh = lb[:, h * hk:(h + 1) * hk]
            om = 1.0 - lbh
            qs = _silu(q)
            sig = jax.nn.sigmoid(z)
            lf = jnp.log(lbh + om * sig)
            k = om * (1.0 - sig)
            if pad:
                lf = jnp.where(row < t_rows, lf, 0.0)
                k = jnp.where(row < t_rows, k, 0.0)
            b = lf
            for s in levels:
                b = b + jnp.where(row >= s, pltpu.roll(b, s, axis=0), 0.0)
            vb = v.astype(BF16)

            a = jnp.where(ri == ci, _dot_nt(qs.astype(BF16), k.astype(BF16)), 0.0)
            e = b
            for s in levels:
                second = (row & s) != 0
                r = jnp.where(second, pltpu.roll(e, s, axis=0), e)
                if 2 * s < bt:
                    e = jnp.where(second, e, pltpu.roll(e, bt - s, axis=0))
                x = (jnp.where(second, qs, k) * jnp.exp(-jnp.abs(b - r))).astype(BF16)
                a = jnp.where(level_mask[s], _dot_nt(x, x), a)

            st = st_ref[i, h]
            qb = (qs * jnp.exp(b)).astype(BF16)
            o = _dot_nt(qb, st.astype(BF16)) + _dot(a.astype(BF16), vb)
            o_ref[rows, h * hv:(h + 1) * hv] = o[0:t_rows]
            bl = b[bt - 1:bt, :]
            kt = (k * jnp.exp(bl - b)).astype(BF16)
            st_ref[i, h] = st * jnp.exp(bl) + _dot_tn(vb, kt)

    for_each_seq(one_seq)

    gate = proj_ref[:, 2 * key + d:2 * key + 2 * d]
    og_ref[...] = (_rmsnorm(o_ref[...], gn_ref[...]) * _silu(gate)).astype(BF16)

    @pl.when(c == pl.num_programs(1) - 1)
    def _():
        def store(i):
            for h in range(n_heads):
                snew_ref[i, h] = st_ref[i, h].T
        for_each_seq(store)


def _gla(proj, s0, lb_logits, gn, snew, *, layer, n_seq, t):
    _, _, n_heads, hk, hv = s0.shape
    d = n_heads * hv
    if t >= GLA_CHUNK:
        nb, t_rows = 1, GLA_CHUNK
    else:
        nb, t_rows = min(GLA_SEQS_PER_STEP, n_seq), t
    bt = max(t_rows, BF16_SUBLANES)
    nc = t // t_rows
    rows = nb * t_rows
    assert t % t_rows == 0 and n_seq % nb == 0 and (nb == 1 or nc == 1) and rows % BF16_SUBLANES == 0
    state_bytes = nb * n_heads * hk * hv * 4
    est = 2 * rows * proj.shape[1] * 4 + 5 * state_bytes + 4 * rows * d * 4 + 8 * 1024 * 1024
    body = functools.partial(_gla_body, layer=layer, nb=nb, t_rows=t_rows, bt=bt,
                             n_heads=n_heads, hk=hk, hv=hv)
    state_spec = pl.BlockSpec((None, nb, n_heads, hk, hv), lambda b, c: (layer, b, 0, 0, 0))
    aliased = () if snew is None else (snew,)
    return pl.pallas_call(
        body,
        grid=(n_seq // nb, nc),
        in_specs=[pl.BlockSpec((rows, proj.shape[1]), lambda b, c: (b * nc + c, 0)),
                  state_spec, _resident(lb_logits.shape), _layer_of(gn, layer)]
                 + [pl.BlockSpec(memory_space=pl.ANY)] * len(aliased),
        out_specs=[pl.BlockSpec((rows, d), lambda b, c: (b * nc + c, 0)), state_spec],
        out_shape=[jax.ShapeDtypeStruct((n_seq * t, d), BF16), jax.ShapeDtypeStruct(s0.shape, F32)],
        input_output_aliases={4: 1} if aliased else {},
        scratch_shapes=[pltpu.VMEM((nb, n_heads, hv, hk), F32), pltpu.VMEM((rows, d), F32)],
        compiler_params=pltpu.CompilerParams(
            dimension_semantics=("parallel", "arbitrary"), vmem_limit_bytes=_vmem_limit(est)),
        name="hgrn_gla",
    )(proj, s0, lb_logits, gn, *aliased)


def _conv_body(x_ref, g_ref, w_ref, wc_ref, buf_ref, a_ref, nbuf_ref, *, nb, tt):
    c = pl.program_id(1)
    d = x_ref.shape[1]

    @pl.when(c == 0)
    def _():
        nbuf_ref[...] = buf_ref[...]

    h = _rmsnorm(x_ref[...], g_ref[...]).astype(BF16)
    gate_b = _dot(h, w_ref[:, 0:d])
    u = (_dot(h, w_ref[:, d:2 * d]) * _dot(h, w_ref[:, 2 * d:3 * d])).reshape(nb, tt, d)
    prev = nbuf_ref[...]
    p2, p1 = prev[:, 0:1, :], prev[:, 1:2, :]
    t = lax.broadcasted_iota(jnp.int32, (nb, tt, d), 1)
    u1 = jnp.where(t == 0, p1, pltpu.roll(u, 1, axis=1))
    u2 = jnp.where(t == 0, p2, jnp.where(t == 1, p1, pltpu.roll(u, 2, axis=1)))
    w0 = wc_ref[0:1, :].reshape(1, 1, d)
    w1 = wc_ref[1:2, :].reshape(1, 1, d)
    w2 = wc_ref[2:3, :].reshape(1, 1, d)
    y = w0 * u2 + w1 * u1 + w2 * u
    a_ref[...] = (gate_b * y.reshape(nb * tt, d)).astype(BF16)
    nbuf_ref[...] = u[:, tt - 2:tt, :]


def _conv_mixer(x, g, w_in, w_conv, buf, *, l, j, n_seq, t):
    r, d = x.shape
    if t >= CONV_ROWS:
        nb, tt = 1, CONV_ROWS
    else:
        nb, tt = min(CONV_ROWS // t, n_seq), t
    nc = t // tt
    rows = nb * tt
    assert t % tt == 0 and n_seq % nb == 0 and (nb == 1 or nc == 1) and tt % 8 == 0 and tt >= CONV_W - 1
    est = 2 * rows * d * 4 + d * 3 * d * 2 + 2 * rows * d * 2 + 10 * rows * d * 4
    buf_spec = pl.BlockSpec((None, nb, CONV_W - 1, d), lambda b, c: (j, b, 0, 0))
    nbuf_spec = pl.BlockSpec((nb, CONV_W - 1, d), lambda b, c: (b, 0, 0))
    return pl.pallas_call(
        functools.partial(_conv_body, nb=nb, tt=tt),
        grid=(n_seq // nb, nc),
        in_specs=[pl.BlockSpec((rows, d), lambda b, c: (b * nc + c, 0)), _layer_of(g, l),
                  _layer_of(w_in, j), _layer_of(w_conv, j), buf_spec],
        out_specs=[pl.BlockSpec((rows, d), lambda b, c: (b * nc + c, 0)), nbuf_spec],
        out_shape=[jax.ShapeDtypeStruct((r, d), BF16), jax.ShapeDtypeStruct(buf.shape[1:], F32)],
        compiler_params=pltpu.CompilerParams(
            dimension_semantics=("parallel", "arbitrary"), vmem_limit_bytes=_vmem_limit(est)),
        name="conv_mixer",
    )(x, g, w_in, w_conv, buf)


def _out_ffn_body(x_ref, a_ref, wo_ref, g_ref, wg_ref, wu_ref, wd_ref, gf_ref, y_ref, *,
                  ff_chunk, final):
    x = x_ref[...] + _dot(a_ref[...], wo_ref[...])
    h = _rmsnorm(x, g_ref[...]).astype(BF16)
    for c in range(0, wg_ref.shape[1], ff_chunk):
        act = (_silu(_dot(h, wg_ref[:, c:c + ff_chunk])) * _dot(h, wu_ref[:, c:c + ff_chunk]))
        x = x + _dot(act.astype(BF16), wd_ref[c:c + ff_chunk, :])
    y_ref[...] = _rmsnorm(x, gf_ref[...]) if final else x


def _out_ffn(x, a, w_out, g, w_gate, w_up, w_down, g_final, *, l, j, tm, final):
    r, d = x.shape
    dff = w_gate.shape[2]
    assert dff % FF_CHUNK == 0
    weights = (d * d + 3 * d * dff) * 2
    est = 2 * tm * d * (4 + 2 + 4) + weights + 3 * tm * d * 4 + tm * FF_CHUNK * 14
    row = lambda i: (i, 0)
    return pl.pallas_call(
        functools.partial(_out_ffn_body, ff_chunk=FF_CHUNK, final=final),
        grid=(r // tm,),
        in_specs=[pl.BlockSpec((tm, d), row), pl.BlockSpec((tm, d), row), _layer_of(w_out, j),
                  _layer_of(g, l), _layer_of(w_gate, l), _layer_of(w_up, l), _layer_of(w_down, l),
                  _resident(g_final.shape)],
        out_specs=pl.BlockSpec((tm, d), row),
        out_shape=jax.ShapeDtypeStruct((r, d), F32),
        compiler_params=pltpu.CompilerParams(
            dimension_semantics=("parallel",), vmem_limit_bytes=_vmem_limit(est)),
        name="out_ffn",
    )(x, a, w_out, g, w_gate, w_up, w_down, g_final)


def _trunk(x, s_hgrn, s_conv, p, *, n_seq, t):
    depth = p["norm_mix"].shape[0]
    tm = min(ROW_TILE, x.shape[0])
    assert x.shape[0] % tm == 0
    new_h, new_c = None, []
    for l in range(depth):
        j = l // 2
        if l % 2 == 0:
            proj = _norm_matmul(x, p["norm_mix"], p["hgrn_w_in"], l=l, j=j, tm=tm)
            a, new_h = _gla(proj, s_hgrn, p["hgrn_lb_logits"], p["hgrn_norm"], new_h,
                            layer=j, n_seq=n_seq, t=t)
            w_out = p["hgrn_w_out"]
        else:
            a, s = _conv_mixer(x, p["norm_mix"], p["conv_w_in"], p["conv_w"], s_conv,
                               l=l, j=j, n_seq=n_seq, t=t)
            new_c.append(s)
            w_out = p["conv_w_out"]
        x = _out_ffn(x, a, w_out, p["norm_ffn"], p["ffn_w_gate"], p["ffn_w_up"], p["ffn_w_down"],
                     p["norm_final"], l=l, j=j, tm=tm, final=(l == depth - 1))
    return x, new_h, jnp.stack(new_c)


def kernel(x_prompt, x_sample, state_hgrn, state_conv, meta_tokens, norm_mix, norm_ffn, norm_final,
           hgrn_w_in, hgrn_w_out, hgrn_lb_logits, hgrn_norm, conv_w_in, conv_w, conv_w_out,
           ffn_w_gate, ffn_w_up, ffn_w_down):
    bsz, seq, d = x_prompt.shape
    dec_b, dec_t, _ = x_sample.shape
    n_meta = meta_tokens.shape[0]
    p = dict(norm_mix=norm_mix[:, None], norm_ffn=norm_ffn[:, None], norm_final=norm_final[None],
             hgrn_lb_logits=hgrn_lb_logits, hgrn_norm=hgrn_norm[:, None], conv_w=conv_w,
             hgrn_w_in=hgrn_w_in.astype(BF16), hgrn_w_out=hgrn_w_out.astype(BF16),
             conv_w_in=conv_w_in.astype(BF16), conv_w_out=conv_w_out.astype(BF16),
             ffn_w_gate=ffn_w_gate.astype(BF16), ffn_w_up=ffn_w_up.astype(BF16),
             ffn_w_down=ffn_w_down.astype(BF16))

    zeros_h = jnp.zeros((state_hgrn.shape[0], 1) + state_hgrn.shape[2:], F32)
    zeros_c = jnp.zeros((state_conv.shape[0], 1) + state_conv.shape[2:], F32)
    _, meta_h, meta_c = _trunk(meta_tokens.astype(F32), zeros_h, zeros_c, p, n_seq=1, t=n_meta)

    prompt_h0 = jnp.broadcast_to(meta_h, (meta_h.shape[0], bsz) + meta_h.shape[2:])
    prompt_c0 = jnp.broadcast_to(meta_c, (meta_c.shape[0], bsz) + meta_c.shape[2:])
    y_prompt, new_hgrn_prompt, new_conv_prompt = _trunk(
        x_prompt.reshape(bsz * seq, d), prompt_h0, prompt_c0, p, n_seq=bsz, t=seq)

    y_sample, new_hgrn_sample, new_conv_sample = _trunk(
        x_sample.reshape(dec_b * dec_t, d), state_hgrn, state_conv, p, n_seq=dec_b, t=dec_t)

    return (y_prompt.reshape(bsz, seq, d), y_sample.reshape(dec_b, dec_t, d),
            new_hgrn_prompt, new_conv_prompt, new_hgrn_sample, new_conv_sample)
```

```python
import functools

import jax
import jax.numpy as jnp
import numpy as np
from jax import lax
from jax.experimental import pallas as pl
from jax.experimental.pallas import tpu as pltpu

F32, BF16 = jnp.float32, jnp.bfloat16
EPS = 1e-6
CONV_W = 3

V7X_VMEM_BYTES = 64 * 1024 * 1024
V7X_VMEM_COMPILER_RESERVE = 6 * 1024 * 1024
BF16_SUBLANES = 16
LANES = 128
V7X_MXU_WIDTH = 256
SCAN_LEVELS = (2, 4)

ROW_TILE = 512
GLA_CHUNK = 64
GLA_LONG_SEQS_PER_STEP = 2
GLA_SEQS_PER_STEP = 8
CONV_ROWS = 512
FF_CHUNK = 1408


def _vmem_limit(est_bytes):
    return int(min(max(est_bytes, 16 * 1024 * 1024), V7X_VMEM_BYTES - V7X_VMEM_COMPILER_RESERVE))


def _rmsnorm(x, g):
    return x * lax.rsqrt(jnp.mean(x * x, axis=-1, keepdims=True) + EPS) * g


def _silu(x):
    return x * jax.nn.sigmoid(x)


def _dot(a, b):
    return jnp.dot(a, b, preferred_element_type=F32)


def _dot_nt(a, b):
    return lax.dot_general(a, b, (((1,), (1,)), ((), ())), preferred_element_type=F32)


def _dot_tn(a, b):
    return lax.dot_general(a, b, (((0,), (0,)), ((), ())), preferred_element_type=F32)


def _resident(shape):
    zeros = (0,) * len(shape)
    return pl.BlockSpec(shape, lambda *_: zeros, pipeline_mode=pl.Buffered(1))


def _layer_of(stacked, l):
    index = (l,) + (0,) * (stacked.ndim - 1)
    return pl.BlockSpec((None,) + stacked.shape[1:], lambda *_: index, pipeline_mode=pl.Buffered(1))


def _norm_matmul_body(x_ref, g_ref, w_ref, o_ref, *, n_chunk):
    h = _rmsnorm(x_ref[...], g_ref[...]).astype(BF16)
    for c in range(0, o_ref.shape[1], n_chunk):
        o_ref[:, c:c + n_chunk] = _dot(h, w_ref[:, c:c + n_chunk])


def _norm_matmul(x, g, w, *, l, j, tm):
    r, d = x.shape
    n = w.shape[2]
    est = 2 * tm * d * 4 + d * n * 2 + 2 * tm * n * 4 + tm * d * 8 + tm * 1024 * 8
    return pl.pallas_call(
        functools.partial(_norm_matmul_body, n_chunk=1024),
        grid=(r // tm,),
        in_specs=[pl.BlockSpec((tm, d), lambda i: (i, 0)), _layer_of(g, l), _layer_of(w, j)],
        out_specs=pl.BlockSpec((tm, n), lambda i: (i, 0)),
        out_shape=jax.ShapeDtypeStruct((r, n), F32),
        compiler_params=pltpu.CompilerParams(
            dimension_semantics=("parallel",), vmem_limit_bytes=_vmem_limit(est)),
        name="hgrn_in_proj",
    )(x, g, w)


def _gla_constants(bt, stack):
    i = np.arange(bt)[:, None]
    t = np.arange(bt)[None, :]
    blocks = [t <= i]
    for s in SCAN_LEVELS:
        mid = (i // (2 * s)) * 2 * s + s - 1
        blocks.append(np.where(i <= mid, (t > i) & (t <= mid), (t > mid) & (t <= i)))
    scan = np.concatenate(blocks, axis=0).astype(np.float32)
    r = np.arange(stack)[:, None]
    c = np.arange(LANES)[None, :] + (r // LANES) * LANES
    x = (r % bt) ^ (c % bt)
    lvl = np.where(x == 0, 0, 1 + np.floor(np.log2(np.maximum(x, 1))).astype(np.int32))
    lvl = np.where((r // bt == c // bt) & (r >= c), lvl, -1).astype(np.int32)
    return jnp.asarray(np.concatenate([scan, scan], axis=1), BF16), jnp.asarray(lvl)


def _gla_body(proj_ref, s0_ref, lbl_ref, gn_ref, scan_ref, lvl_ref, *rest,
              layer, nb, t_rows, bt, n_heads, hk, hv, unrolled):
    og_ref, snew_ref, st_ref, o_ref, w_ref = rest[-5:]
    c = pl.program_id(1)
    key = n_heads * hk
    d = n_heads * hv
    pad = bt - t_rows
    stack = lvl_ref.shape[0]
    group = stack // bt
    width = group * hk
    tiles = stack // LANES

    def for_each_seq(fn):
        if unrolled:
            for i in range(nb):
                fn(i)
        else:
            lax.fori_loop(0, nb, lambda i, _: (fn(i), 0)[1], 0)

    def proj_rows(i, lo, hi):
        if unrolled:
            return proj_ref[i, :, lo:hi]
        return proj_ref[pl.ds(pl.multiple_of(i * t_rows, t_rows), t_rows), lo:hi]

    @pl.when(c == 0)
    def _():
        def load(i):
            for h in range(n_heads):
                st_ref[i, h] = s0_ref[i, h].T
        for_each_seq(load)

    logits = lbl_ref[...]
    ex = jnp.exp(logits - jnp.max(logits, axis=0, keepdims=True))
    sm = ex / jnp.sum(ex, axis=0, keepdims=True)
    lb = jnp.sum(sm[0:layer + 1], axis=0, keepdims=True) - sm[0:1]

    om = 1.0 - lb

    sub = lax.broadcasted_iota(jnp.int32, (1, 8, width), 1)
    lvl = lvl_ref[...]

    def padded(x):
        return x if pad == 0 else jnp.concatenate([x, jnp.zeros((pad, x.shape[1]), x.dtype)], axis=0)

    def by_role(s, second_half, first_half):
        shape = second_half.shape
        pick = jnp.where((sub & s) != 0, second_half.reshape(bt // 8, 8, width),
                         first_half.reshape(bt // 8, 8, width))
        return pick.reshape(shape)

    def stacked(x):
        return jnp.concatenate([x[:, e * hk:(e + 1) * hk] for e in range(group)], axis=0).astype(BF16)

    def gram(x, y):
        full = _dot_nt(x, y)
        return jnp.concatenate(
            [full[n * LANES:(n + 1) * LANES, n * LANES:(n + 1) * LANES] for n in range(tiles)], axis=0)

    n_stacks = n_heads // group

    def prepare(i):
        wv = w_ref.at[i if unrolled else 0]
        q = padded(proj_rows(i, 0, key))
        z = padded(proj_rows(i, key, 2 * key))
        w = om * jax.nn.sigmoid(z)
        f = lb + w
        lf = jnp.log(f)
        k = om - w
        if pad:
            real = lax.broadcasted_iota(jnp.int32, (bt, key), 0) < t_rows
            f = jnp.where(real, f, 1.0)
            lf = jnp.where(real, lf, 0.0)
            k = jnp.where(real, k, 0.0)
        hi = lf.astype(BF16)
        lo = (lf - hi.astype(F32)).astype(BF16)
        wv[0] = _silu(q)
        wv[1] = k
        wv[2] = f
        wv[3:3 + 1 + len(SCAN_LEVELS)] = _dot(
            scan_ref[...], jnp.concatenate([hi, lo], axis=0)).reshape(1 + len(SCAN_LEVELS), bt, key)

    def intra(i, p):
        wv = w_ref.at[i if unrolled else 0]
        lanes = slice(p * width, (p + 1) * width)
        qs, k, f, b = wv[0, :, lanes], wv[1, :, lanes], wv[2, :, lanes], wv[3, :, lanes]
        v = stacked(padded(proj_rows(i, 2 * key + p * width, 2 * key + (p + 1) * width)))

        operands = [by_role(1, qs * f, k)]
        for n, s in enumerate(SCAN_LEVELS):
            operands.append(by_role(s, qs, k) * jnp.exp(wv[4 + n, :, lanes]))
        s = 2 * SCAN_LEVELS[-1]
        while s < bt:
            pieces = []
            for m in range(0, bt, 2 * s):
                mid = b[m + s - 1:m + s]
                pieces.append(k[m:m + s] * jnp.exp(mid - b[m:m + s]))
                pieces.append(qs[m + s:m + 2 * s] * jnp.exp(b[m + s:m + 2 * s] - mid))
            operands.append(jnp.concatenate(pieces, axis=0))
            s *= 2

        a = jnp.where(lvl == 0, gram(stacked(qs), stacked(k)), 0.0)
        for n, x in enumerate(operands):
            x = stacked(x)
            a = jnp.where(lvl == n + 1, gram(x, x), a)
        bl = b[bt - 1:bt]
        return (a.astype(BF16), v, (qs * jnp.exp(b)).astype(BF16), (k * jnp.exp(bl - b)).astype(BF16),
                jnp.exp(bl))

    def finish(i, p, a, v, qb, kt, decay):
        rows = pl.ds(i * t_rows if unrolled else pl.multiple_of(i * t_rows, t_rows), t_rows)
        o_intra = jnp.concatenate(
            [_dot(a[n * LANES:(n + 1) * LANES], v[n * LANES:(n + 1) * LANES]) for n in range(tiles)],
            axis=0)
        for e in range(group):
            h = p * group + e
            head = slice(e * hk, (e + 1) * hk)
            st = st_ref[i, h]
            o = o_intra[e * bt:(e + 1) * bt] + _dot_nt(qb[:, head], st.astype(BF16))
            o_ref[rows, h * hv:(h + 1) * hv] = o[0:t_rows]
            st_ref[i, h] = st * decay[:, head] + _dot_tn(v[e * bt:(e + 1) * bt], kt[:, head])

    def one_seq(i):
        prepare(i)
        for p in range(n_stacks):
            finish(i, p, *intra(i, p))

    if unrolled:
        prepare(0)
        pending = None
        for i in range(nb):
            for p in range(n_stacks):
                if p == 0 and i + 1 < nb:
                    prepare(i + 1)
                current = (i, p) + intra(i, p)
                if pending is not None:
                    finish(*pending)
                pending = current
        finish(*pending)
    else:
        lax.fori_loop(0, nb, lambda i, _: (one_seq(i), 0)[1], 0)

    gate_lanes = slice(2 * key + d, 2 * key + 2 * d)
    gate = proj_ref[:, :, gate_lanes].reshape(nb * t_rows, d) if unrolled else proj_ref[:, gate_lanes]
    og = (_rmsnorm(o_ref[...], gn_ref[...]) * _silu(gate)).astype(BF16)
    og_ref[...] = og.reshape(og_ref.shape)

    @pl.when(c == pl.num_programs(1) - 1)
    def _():
        def store(i):
            for h in range(n_heads):
                snew_ref[i, h] = st_ref[i, h].T
        for_each_seq(store)


def _gla(proj, s0, lb_logits, gn, snew, *, layer, n_seq, t):
    _, _, n_heads, hk, hv = s0.shape
    d = n_heads * hv
    key = n_heads * hk
    unrolled = t >= GLA_CHUNK
    if unrolled:
        nb, t_rows = min(GLA_LONG_SEQS_PER_STEP, n_seq), GLA_CHUNK
    else:
        nb, t_rows = min(GLA_SEQS_PER_STEP, n_seq), t
    bt = max(t_rows, BF16_SUBLANES)
    nc = t // t_rows
    rows = nb * t_rows
    assert t % t_rows == 0 and n_seq % nb == 0 and (unrolled or nc == 1) and rows % BF16_SUBLANES == 0
    stack = min(V7X_MXU_WIDTH, n_heads * bt)
    assert LANES % bt == 0 and stack % LANES == 0 and (n_heads * bt) % stack == 0 and hk == LANES
    assert bt > 2 * SCAN_LEVELS[-1]
    scan, lvl = _gla_constants(bt, stack)
    n_planes = 4 + len(SCAN_LEVELS)
    state_bytes = nb * n_heads * hk * hv * 4
    est = (2 * rows * proj.shape[1] * 4 + 5 * state_bytes + 4 * rows * d * 4
           + 3 * nb * n_planes * bt * key * 4 + 8 * 1024 * 1024)
    body = functools.partial(_gla_body, layer=layer, nb=nb, t_rows=t_rows, bt=bt,
                             n_heads=n_heads, hk=hk, hv=hv, unrolled=unrolled)
    if unrolled:
        proj = proj.reshape(n_seq, t, proj.shape[1])
        row_block = lambda width: pl.BlockSpec((nb, t_rows, width), lambda b, c: (b, c, 0))
        og_shape = (n_seq, t, d)
    else:
        row_block = lambda width: pl.BlockSpec((rows, width), lambda b, c: (b, 0))
        og_shape = (n_seq * t, d)
    state_spec = pl.BlockSpec((None, nb, n_heads, hk, hv), lambda b, c: (layer, b, 0, 0, 0))
    aliased = () if snew is None else (snew,)
    og, snew = pl.pallas_call(
        body,
        grid=(n_seq // nb, nc),
        in_specs=[row_block(proj.shape[-1]),
                  state_spec, _resident(lb_logits.shape), _layer_of(gn, layer),
                  _resident(scan.shape), _resident(lvl.shape)]
                 + [pl.BlockSpec(memory_space=pl.ANY)] * len(aliased),
        out_specs=[row_block(d), state_spec],
        out_shape=[jax.ShapeDtypeStruct(og_shape, BF16), jax.ShapeDtypeStruct(s0.shape, F32)],
        input_output_aliases={6: 1} if aliased else {},
        scratch_shapes=[pltpu.VMEM((nb, n_heads, hv, hk), F32), pltpu.VMEM((rows, d), F32),
                        pltpu.VMEM((nb if unrolled else 1, n_planes, bt, key), F32)],
        compiler_params=pltpu.CompilerParams(
            dimension_semantics=("parallel", "arbitrary"), vmem_limit_bytes=_vmem_limit(est)),
        name="hgrn_gla",
    )(proj, s0, lb_logits, gn, scan, lvl, *aliased)
    return og.reshape(n_seq * t, d), snew


def _conv_body(x_ref, g_ref, w_ref, wc_ref, buf_ref, a_ref, nbuf_ref, *, nb, tt):
    c = pl.program_id(1)
    d = x_ref.shape[1]

    @pl.when(c == 0)
    def _():
        nbuf_ref[...] = buf_ref[...]

    h = _rmsnorm(x_ref[...], g_ref[...]).astype(BF16)
    gate_b = _dot(h, w_ref[:, 0:d])
    u = (_dot(h, w_ref[:, d:2 * d]) * _dot(h, w_ref[:, 2 * d:3 * d])).reshape(nb, tt, d)
    prev = nbuf_ref[...]
    p2, p1 = prev[:, 0:1, :], prev[:, 1:2, :]
    t = lax.broadcasted_iota(jnp.int32, (nb, tt, d), 1)
    u1 = jnp.where(t == 0, p1, pltpu.roll(u, 1, axis=1))
    u2 = jnp.where(t == 0, p2, jnp.where(t == 1, p1, pltpu.roll(u, 2, axis=1)))
    w0 = wc_ref[0:1, :].reshape(1, 1, d)
    w1 = wc_ref[1:2, :].reshape(1, 1, d)
    w2 = wc_ref[2:3, :].reshape(1, 1, d)
    y = w0 * u2 + w1 * u1 + w2 * u
    a_ref[...] = (gate_b * y.reshape(nb * tt, d)).astype(BF16)
    nbuf_ref[...] = u[:, tt - 2:tt, :]


def _conv_mixer(x, g, w_in, w_conv, buf, *, l, j, n_seq, t):
    r, d = x.shape
    if t >= CONV_ROWS:
        nb, tt = 1, CONV_ROWS
    else:
        nb, tt = min(CONV_ROWS // t, n_seq), t
    nc = t // tt
    rows = nb * tt
    assert t % tt == 0 and n_seq % nb == 0 and (nb == 1 or nc == 1) and tt % 8 == 0 and tt >= CONV_W - 1
    est = 2 * rows * d * 4 + d * 3 * d * 2 + 2 * rows * d * 2 + 10 * rows * d * 4
    buf_spec = pl.BlockSpec((None, nb, CONV_W - 1, d), lambda b, c: (j, b, 0, 0))
    nbuf_spec = pl.BlockSpec((nb, CONV_W - 1, d), lambda b, c: (b, 0, 0))
    return pl.pallas_call(
        functools.partial(_conv_body, nb=nb, tt=tt),
        grid=(n_seq // nb, nc),
        in_specs=[pl.BlockSpec((rows, d), lambda b, c: (b * nc + c, 0)), _layer_of(g, l),
                  _layer_of(w_in, j), _layer_of(w_conv, j), buf_spec],
        out_specs=[pl.BlockSpec((rows, d), lambda b, c: (b * nc + c, 0)), nbuf_spec],
        out_shape=[jax.ShapeDtypeStruct((r, d), BF16), jax.ShapeDtypeStruct(buf.shape[1:], F32)],
        compiler_params=pltpu.CompilerParams(
            dimension_semantics=("parallel", "arbitrary"), vmem_limit_bytes=_vmem_limit(est)),
        name="conv_mixer",
    )(x, g, w_in, w_conv, buf)


def _out_ffn_body(x_ref, a_ref, wo_ref, g_ref, wg_ref, wu_ref, wd_ref, gf_ref, y_ref, *,
                  ff_chunk, final):
    x = x_ref[...] + _dot(a_ref[...], wo_ref[...])
    h = _rmsnorm(x, g_ref[...]).astype(BF16)
    for c in range(0, wg_ref.shape[1], ff_chunk):
        act = (_silu(_dot(h, wg_ref[:, c:c + ff_chunk])) * _dot(h, wu_ref[:, c:c + ff_chunk]))
        x = x + _dot(act.astype(BF16), wd_ref[c:c + ff_chunk, :])
    y_ref[...] = _rmsnorm(x, gf_ref[...]) if final else x


def _out_ffn(x, a, w_out, g, w_gate, w_up, w_down, g_final, *, l, j, tm, final):
    r, d = x.shape
    dff = w_gate.shape[2]
    assert dff % FF_CHUNK == 0
    weights = (d * d + 3 * d * dff) * 2
    est = 2 * tm * d * (4 + 2 + 4) + weights + 3 * tm * d * 4 + tm * FF_CHUNK * 14
    row = lambda i: (i, 0)
    return pl.pallas_call(
        functools.partial(_out_ffn_body, ff_chunk=FF_CHUNK, final=final),
        grid=(r // tm,),
        in_specs=[pl.BlockSpec((tm, d), row), pl.BlockSpec((tm, d), row), _layer_of(w_out, j),
                  _layer_of(g, l), _layer_of(w_gate, l), _layer_of(w_up, l), _layer_of(w_down, l),
                  _resident(g_final.shape)],
        out_specs=pl.BlockSpec((tm, d), row),
        out_shape=jax.ShapeDtypeStruct((r, d), F32),
        compiler_params=pltpu.CompilerParams(
            dimension_semantics=("parallel",), vmem_limit_bytes=_vmem_limit(est)),
        name="out_ffn",
    )(x, a, w_out, g, w_gate, w_up, w_down, g_final)


def _trunk(x, s_hgrn, s_conv, p, *, n_seq, t):
    depth = p["norm_mix"].shape[0]
    tm = min(ROW_TILE, x.shape[0])
    assert x.shape[0] % tm == 0
    new_h, new_c = None, []
    for l in range(depth):
        j = l // 2
        if l % 2 == 0:
            proj = _norm_matmul(x, p["norm_mix"], p["hgrn_w_in"], l=l, j=j, tm=tm)
            a, new_h = _gla(proj, s_hgrn, p["hgrn_lb_logits"], p["hgrn_norm"], new_h,
                            layer=j, n_seq=n_seq, t=t)
            w_out = p["hgrn_w_out"]
        else:
            a, s = _conv_mixer(x, p["norm_mix"], p["conv_w_in"], p["conv_w"], s_conv,
                               l=l, j=j, n_seq=n_seq, t=t)
            new_c.append(s)
            w_out = p["conv_w_out"]
        x = _out_ffn(x, a, w_out, p["norm_ffn"], p["ffn_w_gate"], p["ffn_w_up"], p["ffn_w_down"],
                     p["norm_final"], l=l, j=j, tm=tm, final=(l == depth - 1))
    return x, new_h, jnp.stack(new_c)


def kernel(x_prompt, x_sample, state_hgrn, state_conv, meta_tokens, norm_mix, norm_ffn, norm_final,
           hgrn_w_in, hgrn_w_out, hgrn_lb_logits, hgrn_norm, conv_w_in, conv_w, conv_w_out,
           ffn_w_gate, ffn_w_up, ffn_w_down):
    bsz, seq, d = x_prompt.shape
    dec_b, dec_t, _ = x_sample.shape
    n_meta = meta_tokens.shape[0]
    p = dict(norm_mix=norm_mix[:, None], norm_ffn=norm_ffn[:, None], norm_final=norm_final[None],
             hgrn_lb_logits=hgrn_lb_logits, hgrn_norm=hgrn_norm[:, None], conv_w=conv_w,
             hgrn_w_in=hgrn_w_in.astype(BF16), hgrn_w_out=hgrn_w_out.astype(BF16),
             conv_w_in=conv_w_in.astype(BF16), conv_w_out=conv_w_out.astype(BF16),
             ffn_w_gate=ffn_w_gate.astype(BF16), ffn_w_up=ffn_w_up.astype(BF16),
             ffn_w_down=ffn_w_down.astype(BF16))

    zeros_h = jnp.zeros((state_hgrn.shape[0], 1) + state_hgrn.shape[2:], F32)
    zeros_c = jnp.zeros((state_conv.shape[0], 1) + state_conv.shape[2:], F32)
    _, meta_h, meta_c = _trunk(meta_tokens.astype(F32), zeros_h, zeros_c, p, n_seq=1, t=n_meta)

    prompt_h0 = jnp.broadcast_to(meta_h, (meta_h.shape[0], bsz) + meta_h.shape[2:])
    prompt_c0 = jnp.broadcast_to(meta_c, (meta_c.shape[0], bsz) + meta_c.shape[2:])
    y_prompt, new_hgrn_prompt, new_conv_prompt = _trunk(
        x_prompt.reshape(bsz * seq, d), prompt_h0, prompt_c0, p, n_seq=bsz, t=seq)

    y_sample, new_hgrn_sample, new_conv_sample = _trunk(
        x_sample.reshape(dec_b * dec_t, d), state_hgrn, state_conv, p, n_seq=dec_b, t=dec_t)

    return (y_prompt.reshape(bsz, seq, d), y_sample.reshape(dec_b, dec_t, d),
            new_hgrn_prompt, new_conv_prompt, new_hgrn_sample, new_conv_sample)
```

```python
import functools

import jax
import jax.numpy as jnp
import numpy as np
from jax import lax
from jax.experimental import pallas as pl
from jax.experimental.pallas import tpu as pltpu

F32, BF16 = jnp.float32, jnp.bfloat16
EPS = 1e-6
CONV_W = 3

V7X_VMEM_BYTES = 64 * 1024 * 1024
V7X_VMEM_COMPILER_RESERVE = 6 * 1024 * 1024
BF16_SUBLANES = 16
LANES = 128
V7X_MXU_WIDTH = 256
SCAN_LEVELS = (2, 4)

ROW_TILE = 512
GLA_CHUNK = 64
GLA_LONG_SEQS_PER_STEP = 4
GLA_SEQS_PER_STEP = 8
GLA_SHORT_SEQS_INTERLEAVED = 4
CONV_ROWS = 512
FF_CHUNK = 1408


def _vmem_limit(est_bytes):
    return int(min(max(est_bytes, 16 * 1024 * 1024), V7X_VMEM_BYTES - V7X_VMEM_COMPILER_RESERVE))


def _rmsnorm(x, g):
    return x * lax.rsqrt(jnp.mean(x * x, axis=-1, keepdims=True) + EPS) * g


def _silu(x):
    return x * jax.nn.sigmoid(x)


def _dot(a, b):
    return jnp.dot(a, b, preferred_element_type=F32)


def _dot_nt(a, b):
    return lax.dot_general(a, b, (((1,), (1,)), ((), ())), preferred_element_type=F32)


def _dot_tn(a, b):
    return lax.dot_general(a, b, (((0,), (0,)), ((), ())), preferred_element_type=F32)


def _resident(shape):
    zeros = (0,) * len(shape)
    return pl.BlockSpec(shape, lambda *_: zeros, pipeline_mode=pl.Buffered(1))


def _layer_of(stacked, l):
    index = (l,) + (0,) * (stacked.ndim - 1)
    return pl.BlockSpec((None,) + stacked.shape[1:], lambda *_: index, pipeline_mode=pl.Buffered(1))


def _norm_matmul_body(x_ref, g_ref, w_ref, o_ref, *, n_chunk):
    h = _rmsnorm(x_ref[...], g_ref[...]).astype(BF16)
    for c in range(0, o_ref.shape[1], n_chunk):
        o_ref[:, c:c + n_chunk] = _dot(h, w_ref[:, c:c + n_chunk])


def _norm_matmul(x, g, w, *, l, j, tm):
    r, d = x.shape
    n = w.shape[2]
    est = 2 * tm * d * 4 + d * n * 2 + 2 * tm * n * 4 + tm * d * 8 + tm * 1024 * 8
    return pl.pallas_call(
        functools.partial(_norm_matmul_body, n_chunk=1024),
        grid=(r // tm,),
        in_specs=[pl.BlockSpec((tm, d), lambda i: (i, 0)), _layer_of(g, l), _layer_of(w, j)],
        out_specs=pl.BlockSpec((tm, n), lambda i: (i, 0)),
        out_shape=jax.ShapeDtypeStruct((r, n), F32),
        compiler_params=pltpu.CompilerParams(
            dimension_semantics=("parallel",), vmem_limit_bytes=_vmem_limit(est)),
        name="hgrn_in_proj",
    )(x, g, w)


def _gla_constants(bt, stack):
    i = np.arange(bt)[:, None]
    t = np.arange(bt)[None, :]
    blocks = [t <= i]
    for s in SCAN_LEVELS:
        mid = (i // (2 * s)) * 2 * s + s - 1
        blocks.append(np.where(i <= mid, (t > i) & (t <= mid), (t > mid) & (t <= i)))
    scan = np.concatenate(blocks, axis=0).astype(np.float32)
    r = np.arange(stack)[:, None]
    c = np.arange(LANES)[None, :] + (r // LANES) * LANES
    x = (r % bt) ^ (c % bt)
    lvl = np.where(x == 0, 0, 1 + np.floor(np.log2(np.maximum(x, 1))).astype(np.int32))
    lvl = np.where((r // bt == c // bt) & (r >= c), lvl, -1).astype(np.int32)
    return jnp.asarray(np.concatenate([scan, scan], axis=1), BF16), jnp.asarray(lvl)


def _gla_body(proj_ref, s0_ref, lbl_ref, gn_ref, scan_ref, lvl_ref, *rest,
              layer, nb, t_rows, bt, n_heads, hk, hv, unrolled):
    og_ref, snew_ref, aux_ref, o_ref, w_ref = rest[-5:]
    st_ref = aux_ref if unrolled else None
    hl_ref = None if unrolled else aux_ref
    n_slots = w_ref.shape[0]
    c = pl.program_id(1)
    key = n_heads * hk
    d = n_heads * hv
    pad = bt - t_rows
    stack = lvl_ref.shape[0]
    group = stack // bt
    width = group * hk
    tiles = stack // LANES

    def proj_rows(i, lo, hi):
        if unrolled:
            return proj_ref[i, :, lo:hi]
        return proj_ref[pl.ds(pl.multiple_of(i * t_rows, t_rows), t_rows), lo:hi]

    if unrolled:
        @pl.when(c == 0)
        def _():
            for i in range(nb):
                for h in range(n_heads):
                    st_ref[i, h] = s0_ref[i, h].T

    logits = lbl_ref[...]
    ex = jnp.exp(logits - jnp.max(logits, axis=0, keepdims=True))
    sm = ex / jnp.sum(ex, axis=0, keepdims=True)
    lb = jnp.sum(sm[0:layer + 1], axis=0, keepdims=True) - sm[0:1]

    om = 1.0 - lb

    sub = lax.broadcasted_iota(jnp.int32, (1, 8, width), 1)
    lvl = lvl_ref[...]

    def padded(x):
        return x if pad == 0 else jnp.concatenate([x, jnp.zeros((pad, x.shape[1]), x.dtype)], axis=0)

    def by_role(s, second_half, first_half):
        shape = second_half.shape
        pick = jnp.where((sub & s) != 0, second_half.reshape(bt // 8, 8, width),
                         first_half.reshape(bt // 8, 8, width))
        return pick.reshape(shape)

    def stacked(x):
        return jnp.concatenate([x[:, e * hk:(e + 1) * hk] for e in range(group)], axis=0).astype(BF16)

    def gram(x, y):
        full = _dot_nt(x, y)
        return jnp.concatenate(
            [full[n * LANES:(n + 1) * LANES, n * LANES:(n + 1) * LANES] for n in range(tiles)], axis=0)

    n_stacks = n_heads // group

    def prepare(i, slot):
        wv = w_ref.at[slot]
        q = padded(proj_rows(i, 0, key))
        z = padded(proj_rows(i, key, 2 * key))
        w = om * jax.nn.sigmoid(z)
        f = lb + w
        lf = jnp.log(f)
        k = om - w
        if pad:
            real = lax.broadcasted_iota(jnp.int32, (bt, key), 0) < t_rows
            f = jnp.where(real, f, 1.0)
            lf = jnp.where(real, lf, 0.0)
            k = jnp.where(real, k, 0.0)
        hi = lf.astype(BF16)
        hl = jnp.concatenate([hi, (lf - hi.astype(F32)).astype(BF16)], axis=0)
        if not unrolled:
            hl_ref[slot] = hl
        wv[0] = _silu(q)
        wv[1] = k
        wv[2] = f
        wv[3:3 + 1 + len(SCAN_LEVELS)] = _dot(scan_ref[...], hl).reshape(1 + len(SCAN_LEVELS), bt, key)

    def intra(i, slot, p):
        wv = w_ref.at[slot]
        lanes = slice(p * width, (p + 1) * width)
        qs, k, f, b = wv[0, :, lanes], wv[1, :, lanes], wv[2, :, lanes], wv[3, :, lanes]
        v = stacked(padded(proj_rows(i, 2 * key + p * width, 2 * key + (p + 1) * width)))

        operands = [by_role(1, qs * f, k)]
        for n, s in enumerate(SCAN_LEVELS):
            operands.append(by_role(s, qs, k) * jnp.exp(wv[4 + n, :, lanes]))
        s = 2 * SCAN_LEVELS[-1]
        while s < bt:
            pieces = []
            for m in range(0, bt, 2 * s):
                mid = b[m + s - 1:m + s]
                pieces.append(k[m:m + s] * jnp.exp(mid - b[m:m + s]))
                pieces.append(qs[m + s:m + 2 * s] * jnp.exp(b[m + s:m + 2 * s] - mid))
            operands.append(jnp.concatenate(pieces, axis=0))
            s *= 2

        a = jnp.where(lvl == 0, gram(stacked(qs), stacked(k)), 0.0)
        for n, x in enumerate(operands):
            x = stacked(x)
            a = jnp.where(lvl == n + 1, gram(x, x), a)
        bl = b[bt - 1:bt]
        return (a.astype(BF16), v, (qs * jnp.exp(b)).astype(BF16), (k * jnp.exp(bl - b)).astype(BF16),
                jnp.exp(bl))

    def finish(i, p, a, v, qb, kt, decay):
        rows = pl.ds(i * t_rows if unrolled else pl.multiple_of(i * t_rows, t_rows), t_rows)
        o_intra = jnp.concatenate(
            [_dot(a[n * LANES:(n + 1) * LANES], v[n * LANES:(n + 1) * LANES]) for n in range(tiles)],
            axis=0)
        for e in range(group):
            h = p * group + e
            head = slice(e * hk, (e + 1) * hk)
            st = st_ref[i, h]
            o = o_intra[e * bt:(e + 1) * bt] + _dot_nt(qb[:, head], st.astype(BF16))
            o_ref[rows, h * hv:(h + 1) * hv] = o[0:t_rows]
            st_ref[i, h] = st * decay[:, head] + _dot_tn(v[e * bt:(e + 1) * bt], kt[:, head])

    def finish_in_place(i, slot, a, v, qb, kt):
        rows = pl.ds(pl.multiple_of(i * t_rows, t_rows), t_rows)
        o_intra = _dot(a, v)
        sum_rows = jnp.concatenate([jnp.ones((2 * bt, hv), BF16), jnp.zeros((2 * bt, hv), BF16)], axis=1)
        for h in range(n_heads):
            head = slice(h * hk, (h + 1) * hk)
            s = s0_ref[i, h]
            o = o_intra[h * bt:(h + 1) * bt] + _dot(qb[:, head], s.astype(BF16))
            o_ref[rows, h * hv:(h + 1) * hv] = o[0:t_rows]
            lhs = jnp.concatenate([hl_ref[slot, :, head], kt[:, head]], axis=0)
            rhs = jnp.concatenate(
                [sum_rows,
                 jnp.concatenate([jnp.zeros((bt, hv), BF16), v[h * bt:(h + 1) * bt]], axis=1)], axis=0)
            upd = _dot_tn(lhs, rhs)
            snew_ref[i, h] = jnp.exp(upd[:, 0:hv]) * s + upd[:, hv:2 * hv]

    def interleaved_seqs(j):
        seqs = [j * n_slots + slot for slot in range(n_slots)]
        for slot, i in enumerate(seqs):
            prepare(i, slot)
        parts = [intra(i, slot, 0)[:4] for slot, i in enumerate(seqs)]
        for slot, i in enumerate(seqs):
            finish_in_place(i, slot, *parts[slot])

    if unrolled:
        prepare(0, 0)
        pending = None
        for i in range(nb):
            for p in range(n_stacks):
                if p == 0 and i + 1 < nb:
                    prepare(i + 1, i + 1)
                current = (i, p) + intra(i, i, p)
                if pending is not None:
                    finish(*pending)
                pending = current
        finish(*pending)
    else:
        lax.fori_loop(0, nb // n_slots, lambda j, _: (interleaved_seqs(j), 0)[1], 0)

    gate_lanes = slice(2 * key + d, 2 * key + 2 * d)
    gate = proj_ref[:, :, gate_lanes].reshape(nb * t_rows, d) if unrolled else proj_ref[:, gate_lanes]
    og = (_rmsnorm(o_ref[...], gn_ref[...]) * _silu(gate)).astype(BF16)
    og_ref[...] = og.reshape(og_ref.shape)

    if unrolled:
        @pl.when(c == pl.num_programs(1) - 1)
        def _():
            for i in range(nb):
                for h in range(n_heads):
                    snew_ref[i, h] = st_ref[i, h].T


def _gla(proj, s0, lb_logits, gn, snew, *, layer, n_seq, t):
    _, _, n_heads, hk, hv = s0.shape
    d = n_heads * hv
    key = n_heads * hk
    unrolled = t >= GLA_CHUNK
    if unrolled:
        nb, t_rows = min(GLA_LONG_SEQS_PER_STEP, n_seq), GLA_CHUNK
    else:
        nb, t_rows = min(GLA_SEQS_PER_STEP, n_seq), t
    bt = max(t_rows, BF16_SUBLANES)
    nc = t // t_rows
    rows = nb * t_rows
    assert t % t_rows == 0 and n_seq % nb == 0 and (unrolled or nc == 1) and rows % BF16_SUBLANES == 0
    stack = min(V7X_MXU_WIDTH, n_heads * bt)
    assert LANES % bt == 0 and stack % LANES == 0 and (n_heads * bt) % stack == 0 and hk == LANES
    assert bt > 2 * SCAN_LEVELS[-1] and (unrolled or stack == n_heads * bt == LANES)
    scan, lvl = _gla_constants(bt, stack)
    n_planes = 4 + len(SCAN_LEVELS)
    n_slots = nb if unrolled else (GLA_SHORT_SEQS_INTERLEAVED if nb % GLA_SHORT_SEQS_INTERLEAVED == 0 else 1)
    state_bytes = nb * n_heads * hk * hv * 4
    est = (2 * rows * proj.shape[1] * 4 + 5 * state_bytes + 4 * rows * d * 4
           + 3 * n_slots * n_planes * bt * key * 4 + 8 * 1024 * 1024)
    body = functools.partial(_gla_body, layer=layer, nb=nb, t_rows=t_rows, bt=bt,
                             n_heads=n_heads, hk=hk, hv=hv, unrolled=unrolled)
    if unrolled:
        proj = proj.reshape(n_seq, t, proj.shape[1])
        row_block = lambda width: pl.BlockSpec((nb, t_rows, width), lambda b, c: (b, c, 0))
        og_shape = (n_seq, t, d)
    else:
        row_block = lambda width: pl.BlockSpec((rows, width), lambda b, c: (b, 0))
        og_shape = (n_seq * t, d)
    state_spec = pl.BlockSpec((None, nb, n_heads, hk, hv), lambda b, c: (layer, b, 0, 0, 0))
    aliased = () if snew is None else (snew,)
    og, snew = pl.pallas_call(
        body,
        grid=(n_seq // nb, nc),
        in_specs=[row_block(proj.shape[-1]),
                  state_spec, _resident(lb_logits.shape), _layer_of(gn, layer),
                  _resident(scan.shape), _resident(lvl.shape)]
                 + [pl.BlockSpec(memory_space=pl.ANY)] * len(aliased),
        out_specs=[row_block(d), state_spec],
        out_shape=[jax.ShapeDtypeStruct(og_shape, BF16), jax.ShapeDtypeStruct(s0.shape, F32)],
        input_output_aliases={6: 1} if aliased else {},
        scratch_shapes=[pltpu.VMEM((nb, n_heads, hv, hk), F32) if unrolled
                        else pltpu.VMEM((n_slots, 2 * bt, key), BF16),
                        pltpu.VMEM((rows, d), F32),
                        pltpu.VMEM((n_slots, n_planes, bt, key), F32)],
        compiler_params=pltpu.CompilerParams(
            dimension_semantics=("parallel", "arbitrary"), vmem_limit_bytes=_vmem_limit(est)),
        name="hgrn_gla",
    )(proj, s0, lb_logits, gn, scan, lvl, *aliased)
    return og.reshape(n_seq * t, d), snew


def _conv_body(x_ref, g_ref, w_ref, wc_ref, buf_ref, a_ref, nbuf_ref, *, nb, tt):
    c = pl.program_id(1)
    d = x_ref.shape[1]

    @pl.when(c == 0)
    def _():
        nbuf_ref[...] = buf_ref[...]

    h = _rmsnorm(x_ref[...], g_ref[...]).astype(BF16)
    gate_b = _dot(h, w_ref[:, 0:d])
    u = (_dot(h, w_ref[:, d:2 * d]) * _dot(h, w_ref[:, 2 * d:3 * d])).reshape(nb, tt, d)
    prev = nbuf_ref[...]
    p2, p1 = prev[:, 0:1, :], prev[:, 1:2, :]
    t = lax.broadcasted_iota(jnp.int32, (nb, tt, d), 1)
    u1 = jnp.where(t == 0, p1, pltpu.roll(u, 1, axis=1))
    u2 = jnp.where(t == 0, p2, jnp.where(t == 1, p1, pltpu.roll(u, 2, axis=1)))
    w0 = wc_ref[0:1, :].reshape(1, 1, d)
    w1 = wc_ref[1:2, :].reshape(1, 1, d)
    w2 = wc_ref[2:3, :].reshape(1, 1, d)
    y = w0 * u2 + w1 * u1 + w2 * u
    a_ref[...] = (gate_b * y.reshape(nb * tt, d)).astype(BF16)
    nbuf_ref[...] = u[:, tt - 2:tt, :]


def _conv_mixer(x, g, w_in, w_conv, buf, *, l, j, n_seq, t):
    r, d = x.shape
    if t >= CONV_ROWS:
        nb, tt = 1, CONV_ROWS
    else:
        nb, tt = min(CONV_ROWS // t, n_seq), t
    nc = t // tt
    rows = nb * tt
    assert t % tt == 0 and n_seq % nb == 0 and (nb == 1 or nc == 1) and tt % 8 == 0 and tt >= CONV_W - 1
    est = 2 * rows * d * 4 + d * 3 * d * 2 + 2 * rows * d * 2 + 10 * rows * d * 4
    buf_spec = pl.BlockSpec((None, nb, CONV_W - 1, d), lambda b, c: (j, b, 0, 0))
    nbuf_spec = pl.BlockSpec((nb, CONV_W - 1, d), lambda b, c: (b, 0, 0))
    return pl.pallas_call(
        functools.partial(_conv_body, nb=nb, tt=tt),
        grid=(n_seq // nb, nc),
        in_specs=[pl.BlockSpec((rows, d), lambda b, c: (b * nc + c, 0)), _layer_of(g, l),
                  _layer_of(w_in, j), _layer_of(w_conv, j), buf_spec],
        out_specs=[pl.BlockSpec((rows, d), lambda b, c: (b * nc + c, 0)), nbuf_spec],
        out_shape=[jax.ShapeDtypeStruct((r, d), BF16), jax.ShapeDtypeStruct(buf.shape[1:], F32)],
        compiler_params=pltpu.CompilerParams(
            dimension_semantics=("parallel", "arbitrary"), vmem_limit_bytes=_vmem_limit(est)),
        name="conv_mixer",
    )(x, g, w_in, w_conv, buf)


def _out_ffn_body(x_ref, a_ref, wo_ref, g_ref, wg_ref, wu_ref, wd_ref, gf_ref, y_ref, *,
                  ff_chunk, final):
    x = x_ref[...] + _dot(a_ref[...], wo_ref[...])
    h = _rmsnorm(x, g_ref[...]).astype(BF16)
    for c in range(0, wg_ref.shape[1], ff_chunk):
        act = (_silu(_dot(h, wg_ref[:, c:c + ff_chunk])) * _dot(h, wu_ref[:, c:c + ff_chunk]))
        x = x + _dot(act.astype(BF16), wd_ref[c:c + ff_chunk, :])
    y_ref[...] = _rmsnorm(x, gf_ref[...]) if final else x


def _out_ffn(x, a, w_out, g, w_gate, w_up, w_down, g_final, *, l, j, tm, final):
    r, d = x.shape
    dff = w_gate.shape[2]
    assert dff % FF_CHUNK == 0
    weights = (d * d + 3 * d * dff) * 2
    est = 2 * tm * d * (4 + 2 + 4) + weights + 3 * tm * d * 4 + tm * FF_CHUNK * 14
    row = lambda i: (i, 0)
    return pl.pallas_call(
        functools.partial(_out_ffn_body, ff_chunk=FF_CHUNK, final=final),
        grid=(r // tm,),
        in_specs=[pl.BlockSpec((tm, d), row), pl.BlockSpec((tm, d), row), _layer_of(w_out, j),
                  _layer_of(g, l), _layer_of(w_gate, l), _layer_of(w_up, l), _layer_of(w_down, l),
                  _resident(g_final.shape)],
        out_specs=pl.BlockSpec((tm, d), row),
        out_shape=jax.ShapeDtypeStruct((r, d), F32),
        compiler_params=pltpu.CompilerParams(
            dimension_semantics=("parallel",), vmem_limit_bytes=_vmem_limit(est)),
        name="out_ffn",
    )(x, a, w_out, g, w_gate, w_up, w_down, g_final)


def _trunk(x, s_hgrn, s_conv, p, *, n_seq, t):
    depth = p["norm_mix"].shape[0]
    tm = min(ROW_TILE, x.shape[0])
    assert x.shape[0] % tm == 0
    new_h, new_c = None, []
    for l in range(depth):
        j = l // 2
        if l % 2 == 0:
            proj = _norm_matmul(x, p["norm_mix"], p["hgrn_w_in"], l=l, j=j, tm=tm)
            a, new_h = _gla(proj, s_hgrn, p["hgrn_lb_logits"], p["hgrn_norm"], new_h,
                            layer=j, n_seq=n_seq, t=t)
            w_out = p["hgrn_w_out"]
        else:
            a, s = _conv_mixer(x, p["norm_mix"], p["conv_w_in"], p["conv_w"], s_conv,
                               l=l, j=j, n_seq=n_seq, t=t)
            new_c.append(s)
            w_out = p["conv_w_out"]
        x = _out_ffn(x, a, w_out, p["norm_ffn"], p["ffn_w_gate"], p["ffn_w_up"], p["ffn_w_down"],
                     p["norm_final"], l=l, j=j, tm=tm, final=(l == depth - 1))
    return x, new_h, jnp.stack(new_c)


def kernel(x_prompt, x_sample, state_hgrn, state_conv, meta_tokens, norm_mix, norm_ffn, norm_final,
           hgrn_w_in, hgrn_w_out, hgrn_lb_logits, hgrn_norm, conv_w_in, conv_w, conv_w_out,
           ffn_w_gate, ffn_w_up, ffn_w_down):
    bsz, seq, d = x_prompt.shape
    dec_b, dec_t, _ = x_sample.shape
    n_meta = meta_tokens.shape[0]
    p = dict(norm_mix=norm_mix[:, None], norm_ffn=norm_ffn[:, None], norm_final=norm_final[None],
             hgrn_lb_logits=hgrn_lb_logits, hgrn_norm=hgrn_norm[:, None], conv_w=conv_w,
             hgrn_w_in=hgrn_w_in.astype(BF16), hgrn_w_out=hgrn_w_out.astype(BF16),
             conv_w_in=conv_w_in.astype(BF16), conv_w_out=conv_w_out.astype(BF16),
             ffn_w_gate=ffn_w_gate.astype(BF16), ffn_w_up=ffn_w_up.astype(BF16),
             ffn_w_down=ffn_w_down.astype(BF16))

    zeros_h = jnp.zeros((state_hgrn.shape[0], 1) + state_hgrn.shape[2:], F32)
    zeros_c = jnp.zeros((state_conv.shape[0], 1) + state_conv.shape[2:], F32)
    _, meta_h, meta_c = _trunk(meta_tokens.astype(F32), zeros_h, zeros_c, p, n_seq=1, t=n_meta)

    prompt_h0 = jnp.broadcast_to(meta_h, (meta_h.shape[0], bsz) + meta_h.shape[2:])
    prompt_c0 = jnp.broadcast_to(meta_c, (meta_c.shape[0], bsz) + meta_c.shape[2:])
    y_prompt, new_hgrn_prompt, new_conv_prompt = _trunk(
        x_prompt.reshape(bsz * seq, d), prompt_h0, prompt_c0, p, n_seq=bsz, t=seq)

    y_sample, new_hgrn_sample, new_conv_sample = _trunk(
        x_sample.reshape(dec_b * dec_t, d), state_hgrn, state_conv, p, n_seq=dec_b, t=dec_t)

    return (y_prompt.reshape(bsz, seq, d), y_sample.reshape(dec_b, dec_t, d),
            new_hgrn_prompt, new_conv_prompt, new_hgrn_sample, new_conv_sample)
```

```python
import functools

import jax
import jax.numpy as jnp
import numpy as np
from jax import lax
from jax.experimental import pallas as pl
from jax.experimental.pallas import tpu as pltpu

F32, BF16 = jnp.float32, jnp.bfloat16
EPS = 1e-6
CONV_W = 3

V7X_VMEM_BYTES = 64 * 1024 * 1024
V7X_VMEM_COMPILER_RESERVE = 6 * 1024 * 1024
BF16_SUBLANES = 16
LANES = 128
V7X_MXU_WIDTH = 256
SCAN_LEVELS = (2, 4)

ROW_TILE = 512
GLA_CHUNK = 64
GLA_LONG_SEQS_PER_STEP = 4
GLA_SEQS_PER_STEP = 8
GLA_SHORT_SEQS_INTERLEAVED = 4
CONV_ROWS = 512
CONV_COLS = 256
FF_CHUNKS = 2


def _vmem_limit(est_bytes):
    return int(min(max(est_bytes, 16 * 1024 * 1024), V7X_VMEM_BYTES - V7X_VMEM_COMPILER_RESERVE))


def _rmsnorm(x, g):
    return x * lax.rsqrt(jnp.mean(x * x, axis=-1, keepdims=True) + EPS) * g


def _silu(x):
    return x * jax.nn.sigmoid(x)


def _dot(a, b):
    return jnp.dot(a, b, preferred_element_type=F32)


def _dot_nt(a, b):
    return lax.dot_general(a, b, (((1,), (1,)), ((), ())), preferred_element_type=F32)


def _dot_tn(a, b):
    return lax.dot_general(a, b, (((0,), (0,)), ((), ())), preferred_element_type=F32)


def _resident(shape):
    zeros = (0,) * len(shape)
    return pl.BlockSpec(shape, lambda *_: zeros, pipeline_mode=pl.Buffered(1))


def _layer_of(stacked, l):
    index = (l,) + (0,) * (stacked.ndim - 1)
    return pl.BlockSpec((None,) + stacked.shape[1:], lambda *_: index, pipeline_mode=pl.Buffered(1))


def _norm_matmul_body(x_ref, g_ref, w_ref, o_ref, *, n_chunk):
    h = _rmsnorm(x_ref[...], g_ref[...]).astype(BF16)
    for c in range(0, o_ref.shape[1], n_chunk):
        o_ref[:, c:c + n_chunk] = _dot(h, w_ref[:, c:c + n_chunk])


def _norm_matmul(x, g, w, *, l, j, tm):
    r, d = x.shape
    n = w.shape[2]
    est = 2 * tm * d * 4 + d * n * 2 + 2 * tm * n * 4 + tm * d * 8 + tm * 1024 * 8
    return pl.pallas_call(
        functools.partial(_norm_matmul_body, n_chunk=1024),
        grid=(r // tm,),
        in_specs=[pl.BlockSpec((tm, d), lambda i: (i, 0)), _layer_of(g, l), _layer_of(w, j)],
        out_specs=pl.BlockSpec((tm, n), lambda i: (i, 0)),
        out_shape=jax.ShapeDtypeStruct((r, n), F32),
        compiler_params=pltpu.CompilerParams(
            dimension_semantics=("parallel",), vmem_limit_bytes=_vmem_limit(est)),
        name="hgrn_in_proj",
    )(x, g, w)


def _gla_constants(bt, stack):
    i = np.arange(bt)[:, None]
    t = np.arange(bt)[None, :]
    blocks = [t <= i]
    for s in SCAN_LEVELS:
        mid = (i // (2 * s)) * 2 * s + s - 1
        blocks.append(np.where(i <= mid, (t > i) & (t <= mid), (t > mid) & (t <= i)))
    scan = np.concatenate(blocks, axis=0).astype(np.float32)
    r = np.arange(stack)[:, None]
    c = np.arange(LANES)[None, :] + (r // LANES) * LANES
    x = (r % bt) ^ (c % bt)
    lvl = np.where(x == 0, 0, 1 + np.floor(np.log2(np.maximum(x, 1))).astype(np.int32))
    lvl = np.where((r // bt == c // bt) & (r >= c), lvl, -1).astype(np.int32)
    return jnp.asarray(np.concatenate([scan, scan], axis=1), BF16), jnp.asarray(lvl)


def _gla_body(proj_ref, s0_ref, lbl_ref, gn_ref, scan_ref, lvl_ref, *rest,
              layer, nb, t_rows, bt, n_heads, hk, hv, unrolled):
    og_ref, snew_ref, aux_ref, o_ref, w_ref = rest[-5:]
    st_ref = aux_ref if unrolled else None
    hl_ref = None if unrolled else aux_ref
    n_slots = w_ref.shape[0]
    c = pl.program_id(1)
    key = n_heads * hk
    d = n_heads * hv
    pad = bt - t_rows
    stack = lvl_ref.shape[0]
    group = stack // bt
    width = group * hk
    tiles = stack // LANES

    def proj_rows(i, lo, hi):
        if unrolled:
            return proj_ref[i, :, lo:hi]
        return proj_ref[pl.ds(pl.multiple_of(i * t_rows, t_rows), t_rows), lo:hi]

    if unrolled:
        @pl.when(c == 0)
        def _():
            for i in range(nb):
                for h in range(n_heads):
                    st_ref[i, h] = s0_ref[i, h].T

    logits = lbl_ref[...]
    ex = jnp.exp(logits - jnp.max(logits, axis=0, keepdims=True))
    sm = ex / jnp.sum(ex, axis=0, keepdims=True)
    lb = jnp.sum(sm[0:layer + 1], axis=0, keepdims=True) - sm[0:1]

    om = 1.0 - lb

    sub = lax.broadcasted_iota(jnp.int32, (1, 8, width), 1)
    lvl = lvl_ref[...]

    def padded(x):
        return x if pad == 0 else jnp.concatenate([x, jnp.zeros((pad, x.shape[1]), x.dtype)], axis=0)

    def by_role(s, second_half, first_half):
        shape = second_half.shape
        pick = jnp.where((sub & s) != 0, second_half.reshape(bt // 8, 8, width),
                         first_half.reshape(bt // 8, 8, width))
        return pick.reshape(shape)

    def stacked(x):
        return jnp.concatenate([x[:, e * hk:(e + 1) * hk] for e in range(group)], axis=0).astype(BF16)

    def gram(x, y):
        full = _dot_nt(x, y)
        return jnp.concatenate(
            [full[n * LANES:(n + 1) * LANES, n * LANES:(n + 1) * LANES] for n in range(tiles)], axis=0)

    n_stacks = n_heads // group

    def prepare(i, slot):
        wv = w_ref.at[slot]
        q = padded(proj_rows(i, 0, key))
        z = padded(proj_rows(i, key, 2 * key))
        w = om * jax.nn.sigmoid(z)
        f = lb + w
        lf = jnp.log(f)
        k = om - w
        if pad:
            real = lax.broadcasted_iota(jnp.int32, (bt, key), 0) < t_rows
            f = jnp.where(real, f, 1.0)
            lf = jnp.where(real, lf, 0.0)
            k = jnp.where(real, k, 0.0)
        hi = lf.astype(BF16)
        hl = jnp.concatenate([hi, (lf - hi.astype(F32)).astype(BF16)], axis=0)
        if not unrolled:
            hl_ref[slot] = hl
        wv[0] = _silu(q)
        wv[1] = k
        wv[2] = f
        wv[3:3 + 1 + len(SCAN_LEVELS)] = _dot(scan_ref[...], hl).reshape(1 + len(SCAN_LEVELS), bt, key)

    def intra(i, slot, p):
        wv = w_ref.at[slot]
        lanes = slice(p * width, (p + 1) * width)
        qs, k, f, b = wv[0, :, lanes], wv[1, :, lanes], wv[2, :, lanes], wv[3, :, lanes]
        v = stacked(padded(proj_rows(i, 2 * key + p * width, 2 * key + (p + 1) * width)))

        operands = [by_role(1, qs * f, k)]
        for n, s in enumerate(SCAN_LEVELS):
            operands.append(by_role(s, qs, k) * jnp.exp(wv[4 + n, :, lanes]))
        s = 2 * SCAN_LEVELS[-1]
        while s < bt:
            pieces = []
            for m in range(0, bt, 2 * s):
                mid = b[m + s - 1:m + s]
                pieces.append(k[m:m + s] * jnp.exp(mid - b[m:m + s]))
                pieces.append(qs[m + s:m + 2 * s] * jnp.exp(b[m + s:m + 2 * s] - mid))
            operands.append(jnp.concatenate(pieces, axis=0))
            s *= 2

        a = jnp.where(lvl == 0, gram(stacked(qs), stacked(k)), 0.0)
        for n, x in enumerate(operands):
            x = stacked(x)
            a = jnp.where(lvl == n + 1, gram(x, x), a)
        bl = b[bt - 1:bt]
        return (a.astype(BF16), v, (qs * jnp.exp(b)).astype(BF16), (k * jnp.exp(bl - b)).astype(BF16),
                jnp.exp(bl))

    def finish(i, p, a, v, qb, kt, decay):
        rows = pl.ds(i * t_rows if unrolled else pl.multiple_of(i * t_rows, t_rows), t_rows)
        o_intra = jnp.concatenate(
            [_dot(a[n * LANES:(n + 1) * LANES], v[n * LANES:(n + 1) * LANES]) for n in range(tiles)],
            axis=0)
        for e in range(group):
            h = p * group + e
            head = slice(e * hk, (e + 1) * hk)
            st = st_ref[i, h]
            o = o_intra[e * bt:(e + 1) * bt] + _dot_nt(qb[:, head], st.astype(BF16))
            o_ref[rows, h * hv:(h + 1) * hv] = o[0:t_rows]
            st_ref[i, h] = st * decay[:, head] + _dot_tn(v[e * bt:(e + 1) * bt], kt[:, head])

    def finish_in_place(i, slot, a, v, qb, kt):
        rows = pl.ds(pl.multiple_of(i * t_rows, t_rows), t_rows)
        o_intra = _dot(a, v)
        sum_rows = jnp.concatenate([jnp.ones((2 * bt, hv), BF16), jnp.zeros((2 * bt, hv), BF16)], axis=1)
        for h in range(n_heads):
            head = slice(h * hk, (h + 1) * hk)
            s = s0_ref[i, h]
            o = o_intra[h * bt:(h + 1) * bt] + _dot(qb[:, head], s.astype(BF16))
            o_ref[rows, h * hv:(h + 1) * hv] = o[0:t_rows]
            lhs = jnp.concatenate([hl_ref[slot, :, head], kt[:, head]], axis=0)
            rhs = jnp.concatenate(
                [sum_rows,
                 jnp.concatenate([jnp.zeros((bt, hv), BF16), v[h * bt:(h + 1) * bt]], axis=1)], axis=0)
            upd = _dot_tn(lhs, rhs)
            snew_ref[i, h] = jnp.exp(upd[:, 0:hv]) * s + upd[:, hv:2 * hv]

    def interleaved_seqs(j):
        seqs = [j * n_slots + slot for slot in range(n_slots)]
        for slot, i in enumerate(seqs):
            prepare(i, slot)
        parts = [intra(i, slot, 0)[:4] for slot, i in enumerate(seqs)]
        for slot, i in enumerate(seqs):
            finish_in_place(i, slot, *parts[slot])

    if unrolled:
        prepare(0, 0)
        pending = None
        for i in range(nb):
            for p in range(n_stacks):
                if p == 0 and i + 1 < nb:
                    prepare(i + 1, i + 1)
                current = (i, p) + intra(i, i, p)
                if pending is not None:
                    finish(*pending)
                pending = current
        finish(*pending)
    else:
        lax.fori_loop(0, nb // n_slots, lambda j, _: (interleaved_seqs(j), 0)[1], 0)

    gate_lanes = slice(2 * key + d, 2 * key + 2 * d)
    gate = proj_ref[:, :, gate_lanes].reshape(nb * t_rows, d) if unrolled else proj_ref[:, gate_lanes]
    og = (_rmsnorm(o_ref[...], gn_ref[...]) * _silu(gate)).astype(BF16)
    og_ref[...] = og.reshape(og_ref.shape)

    if unrolled:
        @pl.when(c == pl.num_programs(1) - 1)
        def _():
            for i in range(nb):
                for h in range(n_heads):
                    snew_ref[i, h] = st_ref[i, h].T


def _gla(proj, s0, lb_logits, gn, snew, *, layer, n_seq, t):
    _, _, n_heads, hk, hv = s0.shape
    d = n_heads * hv
    key = n_heads * hk
    unrolled = t >= GLA_CHUNK
    if unrolled:
        nb, t_rows = min(GLA_LONG_SEQS_PER_STEP, n_seq), GLA_CHUNK
    else:
        nb, t_rows = min(GLA_SEQS_PER_STEP, n_seq), t
    bt = max(t_rows, BF16_SUBLANES)
    nc = t // t_rows
    rows = nb * t_rows
    assert t % t_rows == 0 and n_seq % nb == 0 and (unrolled or nc == 1) and rows % BF16_SUBLANES == 0
    stack = min(V7X_MXU_WIDTH, n_heads * bt)
    assert LANES % bt == 0 and stack % LANES == 0 and (n_heads * bt) % stack == 0 and hk == LANES
    assert bt > 2 * SCAN_LEVELS[-1] and (unrolled or stack == n_heads * bt == LANES)
    scan, lvl = _gla_constants(bt, stack)
    n_planes = 4 + len(SCAN_LEVELS)
    n_slots = nb if unrolled else (GLA_SHORT_SEQS_INTERLEAVED if nb % GLA_SHORT_SEQS_INTERLEAVED == 0 else 1)
    state_bytes = nb * n_heads * hk * hv * 4
    est = (2 * rows * proj.shape[1] * 4 + 5 * state_bytes + 4 * rows * d * 4
           + 3 * n_slots * n_planes * bt * key * 4 + 8 * 1024 * 1024)
    body = functools.partial(_gla_body, layer=layer, nb=nb, t_rows=t_rows, bt=bt,
                             n_heads=n_heads, hk=hk, hv=hv, unrolled=unrolled)
    if unrolled:
        proj = proj.reshape(n_seq, t, proj.shape[1])
        row_block = lambda width: pl.BlockSpec((nb, t_rows, width), lambda b, c: (b, c, 0))
        og_shape = (n_seq, t, d)
    else:
        row_block = lambda width: pl.BlockSpec((rows, width), lambda b, c: (b, 0))
        og_shape = (n_seq * t, d)
    state_spec = pl.BlockSpec((None, nb, n_heads, hk, hv), lambda b, c: (layer, b, 0, 0, 0))
    aliased = () if snew is None else (snew,)
    og, snew = pl.pallas_call(
        body,
        grid=(n_seq // nb, nc),
        in_specs=[row_block(proj.shape[-1]),
                  state_spec, _resident(lb_logits.shape), _layer_of(gn, layer),
                  _resident(scan.shape), _resident(lvl.shape)]
                 + [pl.BlockSpec(memory_space=pl.ANY)] * len(aliased),
        out_specs=[row_block(d), state_spec],
        out_shape=[jax.ShapeDtypeStruct(og_shape, BF16), jax.ShapeDtypeStruct(s0.shape, F32)],
        input_output_aliases={6: 1} if aliased else {},
        scratch_shapes=[pltpu.VMEM((nb, n_heads, hv, hk), F32) if unrolled
                        else pltpu.VMEM((n_slots, 2 * bt, key), BF16),
                        pltpu.VMEM((rows, d), F32),
                        pltpu.VMEM((n_slots, n_planes, bt, key), F32)],
        compiler_params=pltpu.CompilerParams(
            dimension_semantics=("parallel", "arbitrary"), vmem_limit_bytes=_vmem_limit(est)),
        name="hgrn_gla",
    )(proj, s0, lb_logits, gn, scan, lvl, *aliased)
    return og.reshape(n_seq * t, d), snew


def _conv_body(x_ref, g_ref, w_ref, wc_ref, buf_ref, a_ref, nbuf_ref, *, nb, tt):
    c = pl.program_id(1)
    d = x_ref.shape[1]
    rows = nb * tt

    @pl.when(c == 0)
    def _():
        nbuf_ref[...] = buf_ref[...]

    h = _rmsnorm(x_ref[...], g_ref[...]).astype(BF16)
    t = lax.broadcasted_iota(jnp.int32, (nb, tt, CONV_COLS), 1)

    def project(n):
        cols = [slice(k * d + n * CONV_COLS, k * d + (n + 1) * CONV_COLS) for k in range(3)]
        return tuple(_dot(h, w_ref[:, s]) for s in cols)

    def tail(n, gate_b, gate_c, third):
        cols = slice(n * CONV_COLS, (n + 1) * CONV_COLS)
        u = (gate_c * third).reshape(nb, tt, CONV_COLS)
        p2, p1 = nbuf_ref[:, 0:1, cols], nbuf_ref[:, 1:2, cols]
        u1 = jnp.where(t == 0, p1, pltpu.roll(u, 1, axis=1))
        u2 = jnp.where(t == 0, p2, jnp.where(t == 1, p1, pltpu.roll(u, 2, axis=1)))
        w0, w1, w2 = (wc_ref[k:k + 1, cols].reshape(1, 1, CONV_COLS) for k in range(3))
        y = w0 * u2 + w1 * u1 + w2 * u
        a_ref[:, cols] = (gate_b * y.reshape(rows, CONV_COLS)).astype(BF16)
        nbuf_ref[:, :, cols] = u[:, tt - 2:tt, :]

    pending = None
    for n in range(d // CONV_COLS):
        current = (n,) + project(n)
        if pending is not None:
            tail(*pending)
        pending = current
    tail(*pending)


def _conv_mixer(x, g, w_in, w_conv, buf, *, l, j, n_seq, t):
    r, d = x.shape
    if t >= CONV_ROWS:
        nb, tt = 1, CONV_ROWS
    else:
        nb, tt = min(CONV_ROWS // t, n_seq), t
    nc = t // tt
    rows = nb * tt
    assert t % tt == 0 and n_seq % nb == 0 and (nb == 1 or nc == 1) and tt % 8 == 0 and tt >= CONV_W - 1
    est = 2 * rows * d * 4 + d * 3 * d * 2 + 2 * rows * d * 2 + 10 * rows * d * 4
    buf_spec = pl.BlockSpec((None, nb, CONV_W - 1, d), lambda b, c: (j, b, 0, 0))
    nbuf_spec = pl.BlockSpec((nb, CONV_W - 1, d), lambda b, c: (b, 0, 0))
    return pl.pallas_call(
        functools.partial(_conv_body, nb=nb, tt=tt),
        grid=(n_seq // nb, nc),
        in_specs=[pl.BlockSpec((rows, d), lambda b, c: (b * nc + c, 0)), _layer_of(g, l),
                  _layer_of(w_in, j), _layer_of(w_conv, j), buf_spec],
        out_specs=[pl.BlockSpec((rows, d), lambda b, c: (b * nc + c, 0)), nbuf_spec],
        out_shape=[jax.ShapeDtypeStruct((r, d), BF16), jax.ShapeDtypeStruct(buf.shape[1:], F32)],
        compiler_params=pltpu.CompilerParams(
            dimension_semantics=("parallel", "arbitrary"), vmem_limit_bytes=_vmem_limit(est)),
        name="conv_mixer",
    )(x, g, w_in, w_conv, buf)


def _ff_chunks(dff):
    tiles = dff // V7X_MXU_WIDTH
    assert tiles * V7X_MXU_WIDTH == dff
    bounds = [round(n * tiles / FF_CHUNKS) * V7X_MXU_WIDTH for n in range(FF_CHUNKS + 1)]
    return tuple(zip(bounds[:-1], bounds[1:]))


def _out_ffn_body(x_ref, a_ref, wo_ref, g_ref, wg_ref, wu_ref, wd_ref, gf_ref, y_ref, *,
                  ff_chunks, final):
    x = x_ref[...] + _dot(a_ref[...], wo_ref[...])
    h = _rmsnorm(x, g_ref[...]).astype(BF16)
    for lo, hi in ff_chunks:
        act = _silu(_dot(h, wg_ref[:, lo:hi])) * _dot(h, wu_ref[:, lo:hi])
        x = x + _dot(act.astype(BF16), wd_ref[lo:hi, :])
    y_ref[...] = _rmsnorm(x, gf_ref[...]) if final else x


def _out_ffn(x, a, w_out, g, w_gate, w_up, w_down, g_final, *, l, j, tm, final):
    r, d = x.shape
    dff = w_gate.shape[2]
    ff_chunks = _ff_chunks(dff)
    widest = max(hi - lo for lo, hi in ff_chunks)
    weights = (d * d + 3 * d * dff) * 2
    est = 2 * tm * d * (4 + 2 + 4) + weights + 3 * tm * d * 4 + tm * widest * 14
    row = lambda i: (i, 0)
    return pl.pallas_call(
        functools.partial(_out_ffn_body, ff_chunks=ff_chunks, final=final),
        grid=(r // tm,),
        in_specs=[pl.BlockSpec((tm, d), row), pl.BlockSpec((tm, d), row), _layer_of(w_out, j),
                  _layer_of(g, l), _layer_of(w_gate, l), _layer_of(w_up, l), _layer_of(w_down, l),
                  _resident(g_final.shape)],
        out_specs=pl.BlockSpec((tm, d), row),
        out_shape=jax.ShapeDtypeStruct((r, d), F32),
        compiler_params=pltpu.CompilerParams(
            dimension_semantics=("parallel",), vmem_limit_bytes=_vmem_limit(est)),
        name="out_ffn",
    )(x, a, w_out, g, w_gate, w_up, w_down, g_final)


def _trunk(x, s_hgrn, s_conv, p, *, n_seq, t):
    depth = p["norm_mix"].shape[0]
    tm = min(ROW_TILE, x.shape[0])
    assert x.shape[0] % tm == 0
    new_h, new_c = None, []
    for l in range(depth):
        j = l // 2
        if l % 2 == 0:
            proj = _norm_matmul(x, p["norm_mix"], p["hgrn_w_in"], l=l, j=j, tm=tm)
            a, new_h = _gla(proj, s_hgrn, p["hgrn_lb_logits"], p["hgrn_norm"], new_h,
                            layer=j, n_seq=n_seq, t=t)
            w_out = p["hgrn_w_out"]
        else:
            a, s = _conv_mixer(x, p["norm_mix"], p["conv_w_in"], p["conv_w"], s_conv,
                               l=l, j=j, n_seq=n_seq, t=t)
            new_c.append(s)
            w_out = p["conv_w_out"]
        x = _out_ffn(x, a, w_out, p["norm_ffn"], p["ffn_w_gate"], p["ffn_w_up"], p["ffn_w_down"],
                     p["norm_final"], l=l, j=j, tm=tm, final=(l == depth - 1))
    return x, new_h, jnp.stack(new_c)


def kernel(x_prompt, x_sample, state_hgrn, state_conv, meta_tokens, norm_mix, norm_ffn, norm_final,
           hgrn_w_in, hgrn_w_out, hgrn_lb_logits, hgrn_norm, conv_w_in, conv_w, conv_w_out,
           ffn_w_gate, ffn_w_up, ffn_w_down):
    bsz, seq, d = x_prompt.shape
    dec_b, dec_t, _ = x_sample.shape
    n_meta = meta_tokens.shape[0]
    p = dict(norm_mix=norm_mix[:, None], norm_ffn=norm_ffn[:, None], norm_final=norm_final[None],
             hgrn_lb_logits=hgrn_lb_logits, hgrn_norm=hgrn_norm[:, None], conv_w=conv_w,
             hgrn_w_in=hgrn_w_in.astype(BF16), hgrn_w_out=hgrn_w_out.astype(BF16),
             conv_w_in=conv_w_in.astype(BF16), conv_w_out=conv_w_out.astype(BF16),
             ffn_w_gate=ffn_w_gate.astype(BF16), ffn_w_up=ffn_w_up.astype(BF16),
             ffn_w_down=ffn_w_down.astype(BF16))

    zeros_h = jnp.zeros((state_hgrn.shape[0], 1) + state_hgrn.shape[2:], F32)
    zeros_c = jnp.zeros((state_conv.shape[0], 1) + state_conv.shape[2:], F32)
    _, meta_h, meta_c = _trunk(meta_tokens.astype(F32), zeros_h, zeros_c, p, n_seq=1, t=n_meta)

    prompt_h0 = jnp.broadcast_to(meta_h, (meta_h.shape[0], bsz) + meta_h.shape[2:])
    prompt_c0 = jnp.broadcast_to(meta_c, (meta_c.shape[0], bsz) + meta_c.shape[2:])
    y_prompt, new_hgrn_prompt, new_conv_prompt = _trunk(
        x_prompt.reshape(bsz * seq, d), prompt_h0, prompt_c0, p, n_seq=bsz, t=seq)

    y_sample, new_hgrn_sample, new_conv_sample = _trunk(
        x_sample.reshape(dec_b * dec_t, d), state_hgrn, state_conv, p, n_seq=dec_b, t=dec_t)

    return (y_prompt.reshape(bsz, seq, d), y_sample.reshape(dec_b, dec_t, d),
            new_hgrn_prompt, new_conv_prompt, new_hgrn_sample, new_conv_sample)
```

```python
import functools

import jax
import jax.numpy as jnp
import numpy as np
from jax import lax
from jax.experimental import pallas as pl
from jax.experimental.pallas import tpu as pltpu

F32, BF16 = jnp.float32, jnp.bfloat16
EPS = 1e-6
CONV_W = 3

V7X_VMEM_BYTES = 64 * 1024 * 1024
V7X_VMEM_COMPILER_RESERVE = 6 * 1024 * 1024
BF16_SUBLANES = 16
LANES = 128
V7X_MXU_WIDTH = 256
SCAN_LEVELS = (2, 4)

ROW_TILE = 1024
GLA_CHUNK = 64
GLA_LONG_SEQS_PER_STEP = 8
GLA_SEQS_PER_STEP = 16
GLA_SHORT_SEQS_INTERLEAVED = 4
CONV_ROWS = 1024
CONV_COLS = 256
FF_CHUNKS = 4


def _vmem_limit(est_bytes):
    return int(min(max(est_bytes, 16 * 1024 * 1024), V7X_VMEM_BYTES - V7X_VMEM_COMPILER_RESERVE))


def _rmsnorm(x, g):
    return x * lax.rsqrt(jnp.mean(x * x, axis=-1, keepdims=True) + EPS) * g


def _silu(x):
    return x * jax.nn.sigmoid(x)


def _dot(a, b):
    return jnp.dot(a, b, preferred_element_type=F32)


def _dot_nt(a, b):
    return lax.dot_general(a, b, (((1,), (1,)), ((), ())), preferred_element_type=F32)


def _dot_tn(a, b):
    return lax.dot_general(a, b, (((0,), (0,)), ((), ())), preferred_element_type=F32)


def _resident(shape):
    zeros = (0,) * len(shape)
    return pl.BlockSpec(shape, lambda *_: zeros, pipeline_mode=pl.Buffered(1))


def _layer_of(stacked, l):
    index = (l,) + (0,) * (stacked.ndim - 1)
    return pl.BlockSpec((None,) + stacked.shape[1:], lambda *_: index, pipeline_mode=pl.Buffered(1))


def _norm_matmul_body(x_ref, g_ref, w_ref, o_ref, *, n_chunk):
    h = _rmsnorm(x_ref[...], g_ref[...]).astype(BF16)
    for c in range(0, o_ref.shape[1], n_chunk):
        o_ref[:, c:c + n_chunk] = _dot(h, w_ref[:, c:c + n_chunk])


def _norm_matmul(x, g, w, *, l, j, tm):
    r, d = x.shape
    n = w.shape[2]
    est = 2 * tm * d * 4 + d * n * 2 + 2 * tm * n * 4 + tm * d * 8 + tm * 1024 * 8
    return pl.pallas_call(
        functools.partial(_norm_matmul_body, n_chunk=1024),
        grid=(r // tm,),
        in_specs=[pl.BlockSpec((tm, d), lambda i: (i, 0)), _layer_of(g, l), _layer_of(w, j)],
        out_specs=pl.BlockSpec((tm, n), lambda i: (i, 0)),
        out_shape=jax.ShapeDtypeStruct((r, n), F32),
        compiler_params=pltpu.CompilerParams(
            dimension_semantics=("parallel",), vmem_limit_bytes=_vmem_limit(est)),
        name="hgrn_in_proj",
    )(x, g, w)


def _gla_constants(bt, stack):
    i = np.arange(bt)[:, None]
    t = np.arange(bt)[None, :]
    blocks = [t <= i]
    for s in SCAN_LEVELS:
        mid = (i // (2 * s)) * 2 * s + s - 1
        blocks.append(np.where(i <= mid, (t > i) & (t <= mid), (t > mid) & (t <= i)))
    scan = np.concatenate(blocks, axis=0).astype(np.float32)
    r = np.arange(stack)[:, None]
    c = np.arange(LANES)[None, :] + (r // LANES) * LANES
    x = (r % bt) ^ (c % bt)
    lvl = np.where(x == 0, 0, 1 + np.floor(np.log2(np.maximum(x, 1))).astype(np.int32))
    lvl = np.where((r // bt == c // bt) & (r >= c), lvl, -1).astype(np.int32)
    return jnp.asarray(np.concatenate([scan, scan], axis=1), BF16), jnp.asarray(lvl)


def _gla_body(proj_ref, s0_ref, lbl_ref, gn_ref, scan_ref, lvl_ref, *rest,
              layer, nb, t_rows, bt, n_heads, hk, hv, unrolled):
    og_ref, snew_ref, aux_ref, o_ref, w_ref = rest[-5:]
    st_ref = aux_ref if unrolled else None
    hl_ref = None if unrolled else aux_ref
    n_slots = w_ref.shape[0]
    c = pl.program_id(1)
    key = n_heads * hk
    d = n_heads * hv
    pad = bt - t_rows
    stack = lvl_ref.shape[0]
    group = stack // bt
    width = group * hk
    tiles = stack // LANES

    def proj_rows(i, lo, hi):
        if unrolled:
            return proj_ref[i, :, lo:hi]
        return proj_ref[pl.ds(pl.multiple_of(i * t_rows, t_rows), t_rows), lo:hi]

    if unrolled:
        @pl.when(c == 0)
        def _():
            for i in range(nb):
                for h in range(n_heads):
                    st_ref[i, h] = s0_ref[i, h].T

    logits = lbl_ref[...]
    ex = jnp.exp(logits - jnp.max(logits, axis=0, keepdims=True))
    sm = ex / jnp.sum(ex, axis=0, keepdims=True)
    lb = jnp.sum(sm[0:layer + 1], axis=0, keepdims=True) - sm[0:1]

    om = 1.0 - lb

    sub = lax.broadcasted_iota(jnp.int32, (1, 8, width), 1)
    lvl = lvl_ref[...]

    def padded(x):
        return x if pad == 0 else jnp.concatenate([x, jnp.zeros((pad, x.shape[1]), x.dtype)], axis=0)

    def by_role(s, second_half, first_half):
        shape = second_half.shape
        pick = jnp.where((sub & s) != 0, second_half.reshape(bt // 8, 8, width),
                         first_half.reshape(bt // 8, 8, width))
        return pick.reshape(shape)

    def stacked(x):
        return jnp.concatenate([x[:, e * hk:(e + 1) * hk] for e in range(group)], axis=0).astype(BF16)

    def gram(x, y):
        full = _dot_nt(x, y)
        return jnp.concatenate(
            [full[n * LANES:(n + 1) * LANES, n * LANES:(n + 1) * LANES] for n in range(tiles)], axis=0)

    n_stacks = n_heads // group

    def prepare(i, slot):
        wv = w_ref.at[slot]
        q = padded(proj_rows(i, 0, key))
        z = padded(proj_rows(i, key, 2 * key))
        w = om * jax.nn.sigmoid(z)
        f = lb + w
        lf = jnp.log(f)
        k = om - w
        if pad:
            real = lax.broadcasted_iota(jnp.int32, (bt, key), 0) < t_rows
            f = jnp.where(real, f, 1.0)
            lf = jnp.where(real, lf, 0.0)
            k = jnp.where(real, k, 0.0)
        hi = lf.astype(BF16)
        hl = jnp.concatenate([hi, (lf - hi.astype(F32)).astype(BF16)], axis=0)
        if not unrolled:
            hl_ref[slot] = hl
        wv[0] = _silu(q)
        wv[1] = k
        wv[2] = f
        wv[3:3 + 1 + len(SCAN_LEVELS)] = _dot(scan_ref[...], hl).reshape(1 + len(SCAN_LEVELS), bt, key)

    def intra(i, slot, p):
        wv = w_ref.at[slot]
        lanes = slice(p * width, (p + 1) * width)
        qs, k, f, b = wv[0, :, lanes], wv[1, :, lanes], wv[2, :, lanes], wv[3, :, lanes]
        v = stacked(padded(proj_rows(i, 2 * key + p * width, 2 * key + (p + 1) * width)))

        operands = [by_role(1, qs * f, k)]
        for n, s in enumerate(SCAN_LEVELS):
            operands.append(by_role(s, qs, k) * jnp.exp(wv[4 + n, :, lanes]))
        s = 2 * SCAN_LEVELS[-1]
        while s < bt:
            pieces = []
            for m in range(0, bt, 2 * s):
                mid = b[m + s - 1:m + s]
                pieces.append(k[m:m + s] * jnp.exp(mid - b[m:m + s]))
                pieces.append(qs[m + s:m + 2 * s] * jnp.exp(b[m + s:m + 2 * s] - mid))
            operands.append(jnp.concatenate(pieces, axis=0))
            s *= 2

        a = jnp.where(lvl == 0, gram(stacked(qs), stacked(k)), 0.0)
        for n, x in enumerate(operands):
            x = stacked(x)
            a = jnp.where(lvl == n + 1, gram(x, x), a)
        bl = b[bt - 1:bt]
        return (a.astype(BF16), v, (qs * jnp.exp(b)).astype(BF16), (k * jnp.exp(bl - b)).astype(BF16),
                jnp.exp(bl))

    def finish(i, p, a, v, qb, kt, decay):
        rows = pl.ds(i * t_rows if unrolled else pl.multiple_of(i * t_rows, t_rows), t_rows)
        o_intra = jnp.concatenate(
            [_dot(a[n * LANES:(n + 1) * LANES], v[n * LANES:(n + 1) * LANES]) for n in range(tiles)],
            axis=0)
        for e in range(group):
            h = p * group + e
            head = slice(e * hk, (e + 1) * hk)
            st = st_ref[i, h]
            o = o_intra[e * bt:(e + 1) * bt] + _dot_nt(qb[:, head], st.astype(BF16))
            o_ref[rows, h * hv:(h + 1) * hv] = o[0:t_rows]
            st_ref[i, h] = st * decay[:, head] + _dot_tn(v[e * bt:(e + 1) * bt], kt[:, head])

    def finish_in_place(i, slot, a, v, qb, kt):
        rows = pl.ds(pl.multiple_of(i * t_rows, t_rows), t_rows)
        o_intra = _dot(a, v)
        sum_rows = jnp.concatenate([jnp.ones((2 * bt, hv), BF16), jnp.zeros((2 * bt, hv), BF16)], axis=1)
        for h in range(n_heads):
            head = slice(h * hk, (h + 1) * hk)
            s = s0_ref[i, h]
            o = o_intra[h * bt:(h + 1) * bt] + _dot(qb[:, head], s.astype(BF16))
            o_ref[rows, h * hv:(h + 1) * hv] = o[0:t_rows]
            lhs = jnp.concatenate([hl_ref[slot, :, head], kt[:, head]], axis=0)
            rhs = jnp.concatenate(
                [sum_rows,
                 jnp.concatenate([jnp.zeros((bt, hv), BF16), v[h * bt:(h + 1) * bt]], axis=1)], axis=0)
            upd = _dot_tn(lhs, rhs)
            snew_ref[i, h] = jnp.exp(upd[:, 0:hv]) * s + upd[:, hv:2 * hv]

    def interleaved_seqs(j):
        seqs = [j * n_slots + slot for slot in range(n_slots)]
        for slot, i in enumerate(seqs):
            prepare(i, slot)
        parts = [intra(i, slot, 0)[:4] for slot, i in enumerate(seqs)]
        for slot, i in enumerate(seqs):
            finish_in_place(i, slot, *parts[slot])

    if unrolled:
        prepare(0, 0)
        pending = None
        for i in range(nb):
            for p in range(n_stacks):
                if p == 0 and i + 1 < nb:
                    prepare(i + 1, i + 1)
                current = (i, p) + intra(i, i, p)
                if pending is not None:
                    finish(*pending)
                pending = current
        finish(*pending)
    else:
        lax.fori_loop(0, nb // n_slots, lambda j, _: (interleaved_seqs(j), 0)[1], 0)

    gate_lanes = slice(2 * key + d, 2 * key + 2 * d)
    gate = proj_ref[:, :, gate_lanes].reshape(nb * t_rows, d) if unrolled else proj_ref[:, gate_lanes]
    og = (_rmsnorm(o_ref[...], gn_ref[...]) * _silu(gate)).astype(BF16)
    og_ref[...] = og.reshape(og_ref.shape)

    if unrolled:
        @pl.when(c == pl.num_programs(1) - 1)
        def _():
            for i in range(nb):
                for h in range(n_heads):
                    snew_ref[i, h] = st_ref[i, h].T


def _gla(proj, s0, lb_logits, gn, snew, *, layer, n_seq, t):
    _, _, n_heads, hk, hv = s0.shape
    d = n_heads * hv
    key = n_heads * hk
    unrolled = t >= GLA_CHUNK
    if unrolled:
        nb, t_rows = min(GLA_LONG_SEQS_PER_STEP, n_seq), GLA_CHUNK
    else:
        nb, t_rows = min(GLA_SEQS_PER_STEP, n_seq), t
    bt = max(t_rows, BF16_SUBLANES)
    nc = t // t_rows
    rows = nb * t_rows
    assert t % t_rows == 0 and n_seq % nb == 0 and (unrolled or nc == 1) and rows % BF16_SUBLANES == 0
    stack = min(V7X_MXU_WIDTH, n_heads * bt)
    assert LANES % bt == 0 and stack % LANES == 0 and (n_heads * bt) % stack == 0 and hk == LANES
    assert bt > 2 * SCAN_LEVELS[-1] and (unrolled or stack == n_heads * bt == LANES)
    scan, lvl = _gla_constants(bt, stack)
    n_planes = 4 + len(SCAN_LEVELS)
    n_slots = nb if unrolled else (GLA_SHORT_SEQS_INTERLEAVED if nb % GLA_SHORT_SEQS_INTERLEAVED == 0 else 1)
    state_bytes = nb * n_heads * hk * hv * 4
    est = (2 * rows * proj.shape[1] * 4 + 5 * state_bytes + 4 * rows * d * 4
           + 3 * n_slots * n_planes * bt * key * 4 + 8 * 1024 * 1024)
    body = functools.partial(_gla_body, layer=layer, nb=nb, t_rows=t_rows, bt=bt,
                             n_heads=n_heads, hk=hk, hv=hv, unrolled=unrolled)
    if unrolled:
        proj = proj.reshape(n_seq, t, proj.shape[1])
        row_block = lambda width: pl.BlockSpec((nb, t_rows, width), lambda b, c: (b, c, 0))
        og_shape = (n_seq, t, d)
    else:
        row_block = lambda width: pl.BlockSpec((rows, width), lambda b, c: (b, 0))
        og_shape = (n_seq * t, d)
    state_spec = pl.BlockSpec((None, nb, n_heads, hk, hv), lambda b, c: (layer, b, 0, 0, 0))
    aliased = () if snew is None else (snew,)
    og, snew = pl.pallas_call(
        body,
        grid=(n_seq // nb, nc),
        in_specs=[row_block(proj.shape[-1]),
                  state_spec, _resident(lb_logits.shape), _layer_of(gn, layer),
                  _resident(scan.shape), _resident(lvl.shape)]
                 + [pl.BlockSpec(memory_space=pl.ANY)] * len(aliased),
        out_specs=[row_block(d), state_spec],
        out_shape=[jax.ShapeDtypeStruct(og_shape, BF16), jax.ShapeDtypeStruct(s0.shape, F32)],
        input_output_aliases={6: 1} if aliased else {},
        scratch_shapes=[pltpu.VMEM((nb, n_heads, hv, hk), F32) if unrolled
                        else pltpu.VMEM((n_slots, 2 * bt, key), BF16),
                        pltpu.VMEM((rows, d), F32),
                        pltpu.VMEM((n_slots, n_planes, bt, key), F32)],
        compiler_params=pltpu.CompilerParams(
            dimension_semantics=("parallel", "arbitrary"), vmem_limit_bytes=_vmem_limit(est)),
        name="hgrn_gla",
    )(proj, s0, lb_logits, gn, scan, lvl, *aliased)
    return og.reshape(n_seq * t, d), snew


def _conv_body(x_ref, g_ref, w_ref, wc_ref, buf_ref, a_ref, nbuf_ref, *, nb, tt):
    c = pl.program_id(1)
    d = x_ref.shape[1]
    rows = nb * tt

    @pl.when(c == 0)
    def _():
        nbuf_ref[...] = buf_ref[...]

    h = _rmsnorm(x_ref[...], g_ref[...]).astype(BF16)
    t = lax.broadcasted_iota(jnp.int32, (nb, tt, CONV_COLS), 1)

    def project(n):
        cols = [slice(k * d + n * CONV_COLS, k * d + (n + 1) * CONV_COLS) for k in range(3)]
        return tuple(_dot(h, w_ref[:, s]) for s in cols)

    def tail(n, gate_b, gate_c, third):
        cols = slice(n * CONV_COLS, (n + 1) * CONV_COLS)
        u = (gate_c * third).reshape(nb, tt, CONV_COLS)
        p2, p1 = nbuf_ref[:, 0:1, cols], nbuf_ref[:, 1:2, cols]
        u1 = jnp.where(t == 0, p1, pltpu.roll(u, 1, axis=1))
        u2 = jnp.where(t == 0, p2, jnp.where(t == 1, p1, pltpu.roll(u, 2, axis=1)))
        w0, w1, w2 = (wc_ref[k:k + 1, cols].reshape(1, 1, CONV_COLS) for k in range(3))
        y = w0 * u2 + w1 * u1 + w2 * u
        a_ref[:, cols] = (gate_b * y.reshape(rows, CONV_COLS)).astype(BF16)
        nbuf_ref[:, :, cols] = u[:, tt - 2:tt, :]

    pending = None
    for n in range(d // CONV_COLS):
        current = (n,) + project(n)
        if pending is not None:
            tail(*pending)
        pending = current
    tail(*pending)


def _conv_mixer(x, g, w_in, w_conv, buf, *, l, j, n_seq, t):
    r, d = x.shape
    if t >= CONV_ROWS:
        nb, tt = 1, CONV_ROWS
    else:
        nb, tt = min(CONV_ROWS // t, n_seq), t
    nc = t // tt
    rows = nb * tt
    assert t % tt == 0 and n_seq % nb == 0 and (nb == 1 or nc == 1) and tt % 8 == 0 and tt >= CONV_W - 1
    est = 2 * rows * d * 4 + d * 3 * d * 2 + 2 * rows * d * 2 + 10 * rows * d * 4
    buf_spec = pl.BlockSpec((None, nb, CONV_W - 1, d), lambda b, c: (j, b, 0, 0))
    nbuf_spec = pl.BlockSpec((nb, CONV_W - 1, d), lambda b, c: (b, 0, 0))
    return pl.pallas_call(
        functools.partial(_conv_body, nb=nb, tt=tt),
        grid=(n_seq // nb, nc),
        in_specs=[pl.BlockSpec((rows, d), lambda b, c: (b * nc + c, 0)), _layer_of(g, l),
                  _layer_of(w_in, j), _layer_of(w_conv, j), buf_spec],
        out_specs=[pl.BlockSpec((rows, d), lambda b, c: (b * nc + c, 0)), nbuf_spec],
        out_shape=[jax.ShapeDtypeStruct((r, d), BF16), jax.ShapeDtypeStruct(buf.shape[1:], F32)],
        compiler_params=pltpu.CompilerParams(
            dimension_semantics=("parallel", "arbitrary"), vmem_limit_bytes=_vmem_limit(est)),
        name="conv_mixer",
    )(x, g, w_in, w_conv, buf)


def _ff_chunks(dff):
    tiles = dff // V7X_MXU_WIDTH
    assert tiles * V7X_MXU_WIDTH == dff
    bounds = [round(n * tiles / FF_CHUNKS) * V7X_MXU_WIDTH for n in range(FF_CHUNKS + 1)]
    return tuple(zip(bounds[:-1], bounds[1:]))


def _out_ffn_body(x_ref, a_ref, wo_ref, g_ref, wg_ref, wu_ref, wd_ref, gf_ref, y_ref, *,
                  ff_chunks, final):
    x = x_ref[...] + _dot(a_ref[...], wo_ref[...])
    h = _rmsnorm(x, g_ref[...]).astype(BF16)
    for lo, hi in ff_chunks:
        act = _silu(_dot(h, wg_ref[:, lo:hi])) * _dot(h, wu_ref[:, lo:hi])
        x = x + _dot(act.astype(BF16), wd_ref[lo:hi, :])
    y_ref[...] = _rmsnorm(x, gf_ref[...]) if final else x


def _out_ffn(x, a, w_out, g, w_gate, w_up, w_down, g_final, *, l, j, tm, final):
    r, d = x.shape
    dff = w_gate.shape[2]
    ff_chunks = _ff_chunks(dff)
    widest = max(hi - lo for lo, hi in ff_chunks)
    weights = (d * d + 3 * d * dff) * 2
    est = 2 * tm * d * (4 + 2 + 4) + weights + 3 * tm * d * 4 + tm * widest * 14
    row = lambda i: (i, 0)
    return pl.pallas_call(
        functools.partial(_out_ffn_body, ff_chunks=ff_chunks, final=final),
        grid=(r // tm,),
        in_specs=[pl.BlockSpec((tm, d), row), pl.BlockSpec((tm, d), row), _layer_of(w_out, j),
                  _layer_of(g, l), _layer_of(w_gate, l), _layer_of(w_up, l), _layer_of(w_down, l),
                  _resident(g_final.shape)],
        out_specs=pl.BlockSpec((tm, d), row),
        out_shape=jax.ShapeDtypeStruct((r, d), F32),
        compiler_params=pltpu.CompilerParams(
            dimension_semantics=("parallel",), vmem_limit_bytes=_vmem_limit(est)),
        name="out_ffn",
    )(x, a, w_out, g, w_gate, w_up, w_down, g_final)


def _trunk(x, s_hgrn, s_conv, p, *, n_seq, t):
    depth = p["norm_mix"].shape[0]
    tm = min(ROW_TILE, x.shape[0])
    assert x.shape[0] % tm == 0
    new_h, new_c = None, []
    for l in range(depth):
        j = l // 2
        if l % 2 == 0:
            proj = _norm_matmul(x, p["norm_mix"], p["hgrn_w_in"], l=l, j=j, tm=tm)
            a, new_h = _gla(proj, s_hgrn, p["hgrn_lb_logits"], p["hgrn_norm"], new_h,
                            layer=j, n_seq=n_seq, t=t)
            w_out = p["hgrn_w_out"]
        else:
            a, s = _conv_mixer(x, p["norm_mix"], p["conv_w_in"], p["conv_w"], s_conv,
                               l=l, j=j, n_seq=n_seq, t=t)
            new_c.append(s)
            w_out = p["conv_w_out"]
        x = _out_ffn(x, a, w_out, p["norm_ffn"], p["ffn_w_gate"], p["ffn_w_up"], p["ffn_w_down"],
                     p["norm_final"], l=l, j=j, tm=tm, final=(l == depth - 1))
    return x, new_h, jnp.stack(new_c)


def kernel(x_prompt, x_sample, state_hgrn, state_conv, meta_tokens, norm_mix, norm_ffn, norm_final,
           hgrn_w_in, hgrn_w_out, hgrn_lb_logits, hgrn_norm, conv_w_in, conv_w, conv_w_out,
           ffn_w_gate, ffn_w_up, ffn_w_down):
    bsz, seq, d = x_prompt.shape
    dec_b, dec_t, _ = x_sample.shape
    n_meta = meta_tokens.shape[0]
    p = dict(norm_mix=norm_mix[:, None], norm_ffn=norm_ffn[:, None], norm_final=norm_final[None],
             hgrn_lb_logits=hgrn_lb_logits, hgrn_norm=hgrn_norm[:, None], conv_w=conv_w,
             hgrn_w_in=hgrn_w_in.astype(BF16), hgrn_w_out=hgrn_w_out.astype(BF16),
             conv_w_in=conv_w_in.astype(BF16), conv_w_out=conv_w_out.astype(BF16),
             ffn_w_gate=ffn_w_gate.astype(BF16), ffn_w_up=ffn_w_up.astype(BF16),
             ffn_w_down=ffn_w_down.astype(BF16))

    zeros_h = jnp.zeros((state_hgrn.shape[0], 1) + state_hgrn.shape[2:], F32)
    zeros_c = jnp.zeros((state_conv.shape[0], 1) + state_conv.shape[2:], F32)
    _, meta_h, meta_c = _trunk(meta_tokens.astype(F32), zeros_h, zeros_c, p, n_seq=1, t=n_meta)

    prompt_h0 = jnp.broadcast_to(meta_h, (meta_h.shape[0], bsz) + meta_h.shape[2:])
    prompt_c0 = jnp.broadcast_to(meta_c, (meta_c.shape[0], bsz) + meta_c.shape[2:])
    y_prompt, new_hgrn_prompt, new_conv_prompt = _trunk(
        x_prompt.reshape(bsz * seq, d), prompt_h0, prompt_c0, p, n_seq=bsz, t=seq)

    y_sample, new_hgrn_sample, new_conv_sample = _trunk(
        x_sample.reshape(dec_b * dec_t, d), state_hgrn, state_conv, p, n_seq=dec_b, t=dec_t)

    return (y_prompt.reshape(bsz, seq, d), y_sample.reshape(dec_b, dec_t, d),
            new_hgrn_prompt, new_conv_prompt, new_hgrn_sample, new_conv_sample)
```

```python
import functools

import jax
import jax.numpy as jnp
import numpy as np
from jax import lax
from jax.experimental import pallas as pl
from jax.experimental.pallas import tpu as pltpu

F32, BF16 = jnp.float32, jnp.bfloat16
EPS = 1e-6
CONV_W = 3

V7X_VMEM_BYTES = 64 * 1024 * 1024
V7X_VMEM_COMPILER_RESERVE = 6 * 1024 * 1024
BF16_SUBLANES = 16
LANES = 128
V7X_MXU_WIDTH = 256
SCAN_LEVELS = (2, 4)

ROW_TILE = 1024
MIN_PIPELINED_ROWS = 256
GLA_CHUNK = 64
GLA_LONG_SEQS_PER_STEP = 8
GLA_SEQS_PER_STEP = 16
GLA_SHORT_SEQS_INTERLEAVED = 4
CONV_ROWS = 1024
CONV_COLS = 256
FF_CHUNKS = 4


def _vmem_limit(est_bytes):
    return int(min(max(est_bytes, 16 * 1024 * 1024), V7X_VMEM_BYTES - V7X_VMEM_COMPILER_RESERVE))


def _rmsnorm(x, g):
    return x * lax.rsqrt(jnp.mean(x * x, axis=-1, keepdims=True) + EPS) * g


def _silu(x):
    half = 0.5 * x
    return half + half * jnp.tanh(half)


def _dot(a, b):
    return jnp.dot(a, b, preferred_element_type=F32)


def _dot_nt(a, b):
    return lax.dot_general(a, b, (((1,), (1,)), ((), ())), preferred_element_type=F32)


def _dot_tn(a, b):
    return lax.dot_general(a, b, (((0,), (0,)), ((), ())), preferred_element_type=F32)


def _resident(shape):
    zeros = (0,) * len(shape)
    return pl.BlockSpec(shape, lambda *_: zeros, pipeline_mode=pl.Buffered(1))


def _layer_of(stacked, l):
    index = (l,) + (0,) * (stacked.ndim - 1)
    return pl.BlockSpec((None,) + stacked.shape[1:], lambda *_: index, pipeline_mode=pl.Buffered(1))


def _norm_matmul_body(x_ref, g_ref, w_ref, o_ref, *, n_chunk):
    h = _rmsnorm(x_ref[...], g_ref[...]).astype(BF16)
    for c in range(0, o_ref.shape[1], n_chunk):
        o_ref[:, c:c + n_chunk] = _dot(h, w_ref[:, c:c + n_chunk])


def _norm_matmul(x, g, w, *, l, j, tm):
    r, d = x.shape
    n = w.shape[2]
    est = 2 * tm * d * 4 + d * n * 2 + 2 * tm * n * 4 + tm * d * 8 + tm * 1024 * 8
    return pl.pallas_call(
        functools.partial(_norm_matmul_body, n_chunk=1024),
        grid=(r // tm,),
        in_specs=[pl.BlockSpec((tm, d), lambda i: (i, 0)), _layer_of(g, l), _layer_of(w, j)],
        out_specs=pl.BlockSpec((tm, n), lambda i: (i, 0)),
        out_shape=jax.ShapeDtypeStruct((r, n), F32),
        compiler_params=pltpu.CompilerParams(
            dimension_semantics=("parallel",), vmem_limit_bytes=_vmem_limit(est)),
        name="hgrn_in_proj",
    )(x, g, w)


def _gla_constants(bt, stack):
    i = np.arange(bt)[:, None]
    t = np.arange(bt)[None, :]
    blocks = [t <= i]
    for s in SCAN_LEVELS:
        mid = (i // (2 * s)) * 2 * s + s - 1
        blocks.append(np.where(i <= mid, (t > i) & (t <= mid), (t > mid) & (t <= i)))
    scan = np.concatenate(blocks, axis=0).astype(np.float32)
    r = np.arange(stack)[:, None]
    c = np.arange(LANES)[None, :] + (r // LANES) * LANES
    x = (r % bt) ^ (c % bt)
    lvl = np.where(x == 0, 0, 1 + np.floor(np.log2(np.maximum(x, 1))).astype(np.int32))
    lvl = np.where((r // bt == c // bt) & (r >= c), lvl, -1).astype(np.int32)
    return jnp.asarray(np.concatenate([scan, scan], axis=1), BF16), jnp.asarray(lvl)


def _gla_body(proj_ref, s0_ref, lbl_ref, gn_ref, scan_ref, lvl_ref, *rest,
              layer, nb, t_rows, bt, n_heads, hk, hv, unrolled):
    og_ref, snew_ref, aux_ref, o_ref, w_ref = rest[-5:]
    st_ref = aux_ref if unrolled else None
    hl_ref = None if unrolled else aux_ref
    n_slots = w_ref.shape[0]
    c = pl.program_id(1)
    key = n_heads * hk
    d = n_heads * hv
    pad = bt - t_rows
    stack = lvl_ref.shape[0]
    group = stack // bt
    width = group * hk
    tiles = stack // LANES

    def proj_rows(i, lo, hi):
        if unrolled:
            return proj_ref[i, :, lo:hi]
        return proj_ref[pl.ds(pl.multiple_of(i * t_rows, t_rows), t_rows), lo:hi]

    if unrolled:
        @pl.when(c == 0)
        def _():
            for i in range(nb):
                for h in range(n_heads):
                    st_ref[i, h] = s0_ref[i, h].T

    logits = lbl_ref[...]
    ex = jnp.exp(logits - jnp.max(logits, axis=0, keepdims=True))
    sm = ex / jnp.sum(ex, axis=0, keepdims=True)
    lb = jnp.sum(sm[0:layer + 1], axis=0, keepdims=True) - sm[0:1]

    om = 1.0 - lb
    half_om = 0.5 * om

    sub = lax.broadcasted_iota(jnp.int32, (1, 8, width), 1)
    lvl = lvl_ref[...]

    def padded(x):
        return x if pad == 0 else jnp.concatenate([x, jnp.zeros((pad, x.shape[1]), x.dtype)], axis=0)

    def by_role(s, second_half, first_half):
        shape = second_half.shape
        pick = jnp.where((sub & s) != 0, second_half.reshape(bt // 8, 8, width),
                         first_half.reshape(bt // 8, 8, width))
        return pick.reshape(shape)

    def stacked(x):
        return jnp.concatenate([x[:, e * hk:(e + 1) * hk] for e in range(group)], axis=0).astype(BF16)

    def gram(x, y):
        full = _dot_nt(x, y)
        return jnp.concatenate(
            [full[n * LANES:(n + 1) * LANES, n * LANES:(n + 1) * LANES] for n in range(tiles)], axis=0)

    n_stacks = n_heads // group

    def prepare(i, slot):
        wv = w_ref.at[slot]
        q = padded(proj_rows(i, 0, key))
        z = padded(proj_rows(i, key, 2 * key))
        w = half_om + half_om * jnp.tanh(0.5 * z)
        f = lb + w
        lf = jnp.log(f)
        k = om - w
        if pad:
            real = lax.broadcasted_iota(jnp.int32, (bt, key), 0) < t_rows
            f = jnp.where(real, f, 1.0)
            lf = jnp.where(real, lf, 0.0)
            k = jnp.where(real, k, 0.0)
        hi = lf.astype(BF16)
        hl = jnp.concatenate([hi, (lf - hi.astype(F32)).astype(BF16)], axis=0)
        if not unrolled:
            hl_ref[slot] = hl
        wv[0] = _silu(q)
        wv[1] = k
        wv[2] = f
        wv[3:3 + 1 + len(SCAN_LEVELS)] = _dot(scan_ref[...], hl).reshape(1 + len(SCAN_LEVELS), bt, key)

    def intra(i, slot, p):
        wv = w_ref.at[slot]
        lanes = slice(p * width, (p + 1) * width)
        qs, k, f, b = wv[0, :, lanes], wv[1, :, lanes], wv[2, :, lanes], wv[3, :, lanes]
        v = stacked(padded(proj_rows(i, 2 * key + p * width, 2 * key + (p + 1) * width)))

        operands = [by_role(1, qs * f, k)]
        for n, s in enumerate(SCAN_LEVELS):
            operands.append(by_role(s, qs, k) * jnp.exp(wv[4 + n, :, lanes]))
        s = 2 * SCAN_LEVELS[-1]
        while s < bt:
            pieces = []
            for m in range(0, bt, 2 * s):
                mid = b[m + s - 1:m + s]
                pieces.append(k[m:m + s] * jnp.exp(mid - b[m:m + s]))
                pieces.append(qs[m + s:m + 2 * s] * jnp.exp(b[m + s:m + 2 * s] - mid))
            operands.append(jnp.concatenate(pieces, axis=0))
            s *= 2

        a = jnp.where(lvl == 0, gram(stacked(qs), stacked(k)), 0.0)
        for n, x in enumerate(operands):
            x = stacked(x)
            a = jnp.where(lvl == n + 1, gram(x, x), a)
        bl = b[bt - 1:bt]
        return (a.astype(BF16), v, (qs * jnp.exp(b)).astype(BF16), (k * jnp.exp(bl - b)).astype(BF16),
                jnp.exp(bl))

    def finish(i, p, a, v, qb, kt, decay):
        rows = pl.ds(i * t_rows if unrolled else pl.multiple_of(i * t_rows, t_rows), t_rows)
        o_intra = jnp.concatenate(
            [_dot(a[n * LANES:(n + 1) * LANES], v[n * LANES:(n + 1) * LANES]) for n in range(tiles)],
            axis=0)
        for e in range(group):
            h = p * group + e
            head = slice(e * hk, (e + 1) * hk)
            st = st_ref[i, h]
            o = o_intra[e * bt:(e + 1) * bt] + _dot_nt(qb[:, head], st.astype(BF16))
            o_ref[rows, h * hv:(h + 1) * hv] = o[0:t_rows]
            st_ref[i, h] = st * decay[:, head] + _dot_tn(v[e * bt:(e + 1) * bt], kt[:, head])

    def finish_in_place(i, slot, a, v, qb, kt):
        rows = pl.ds(pl.multiple_of(i * t_rows, t_rows), t_rows)
        o_intra = _dot(a, v)
        sum_rows = jnp.concatenate([jnp.ones((2 * bt, hv), BF16), jnp.zeros((2 * bt, hv), BF16)], axis=1)
        for h in range(n_heads):
            head = slice(h * hk, (h + 1) * hk)
            s = s0_ref[i, h]
            o = o_intra[h * bt:(h + 1) * bt] + _dot(qb[:, head], s.astype(BF16))
            o_ref[rows, h * hv:(h + 1) * hv] = o[0:t_rows]
            lhs = jnp.concatenate([hl_ref[slot, :, head], kt[:, head]], axis=0)
            rhs = jnp.concatenate(
                [sum_rows,
                 jnp.concatenate([jnp.zeros((bt, hv), BF16), v[h * bt:(h + 1) * bt]], axis=1)], axis=0)
            upd = _dot_tn(lhs, rhs)
            snew_ref[i, h] = jnp.exp(upd[:, 0:hv]) * s + upd[:, hv:2 * hv]

    def interleaved_seqs(j):
        seqs = [j * n_slots + slot for slot in range(n_slots)]
        for slot, i in enumerate(seqs):
            prepare(i, slot)
        parts = [intra(i, slot, 0)[:4] for slot, i in enumerate(seqs)]
        for slot, i in enumerate(seqs):
            finish_in_place(i, slot, *parts[slot])

    if unrolled:
        prepare(0, 0)
        pending = None
        for i in range(nb):
            for p in range(n_stacks):
                if p == 0 and i + 1 < nb:
                    prepare(i + 1, i + 1)
                current = (i, p) + intra(i, i, p)
                if pending is not None:
                    finish(*pending)
                pending = current
        finish(*pending)
    else:
        lax.fori_loop(0, nb // n_slots, lambda j, _: (interleaved_seqs(j), 0)[1], 0)

    gate_lanes = slice(2 * key + d, 2 * key + 2 * d)
    gate = proj_ref[:, :, gate_lanes].reshape(nb * t_rows, d) if unrolled else proj_ref[:, gate_lanes]
    og = (_rmsnorm(o_ref[...], gn_ref[...]) * _silu(gate)).astype(BF16)
    og_ref[...] = og.reshape(og_ref.shape)

    if unrolled:
        @pl.when(c == pl.num_programs(1) - 1)
        def _():
            for i in range(nb):
                for h in range(n_heads):
                    snew_ref[i, h] = st_ref[i, h].T


def _gla(proj, s0, lb_logits, gn, snew, *, layer, n_seq, t):
    _, _, n_heads, hk, hv = s0.shape
    d = n_heads * hv
    key = n_heads * hk
    unrolled = t >= GLA_CHUNK
    if unrolled:
        nb, t_rows = min(GLA_LONG_SEQS_PER_STEP, n_seq), GLA_CHUNK
    else:
        nb, t_rows = min(GLA_SEQS_PER_STEP, n_seq), t
    bt = max(t_rows, BF16_SUBLANES)
    nc = t // t_rows
    rows = nb * t_rows
    assert t % t_rows == 0 and n_seq % nb == 0 and (unrolled or nc == 1) and rows % BF16_SUBLANES == 0
    stack = min(V7X_MXU_WIDTH, n_heads * bt)
    assert LANES % bt == 0 and stack % LANES == 0 and (n_heads * bt) % stack == 0 and hk == LANES
    assert bt > 2 * SCAN_LEVELS[-1] and (unrolled or stack == n_heads * bt == LANES)
    scan, lvl = _gla_constants(bt, stack)
    n_planes = 4 + len(SCAN_LEVELS)
    n_slots = nb if unrolled else (GLA_SHORT_SEQS_INTERLEAVED if nb % GLA_SHORT_SEQS_INTERLEAVED == 0 else 1)
    state_bytes = nb * n_heads * hk * hv * 4
    est = (2 * rows * proj.shape[1] * 4 + 5 * state_bytes + 4 * rows * d * 4
           + 3 * n_slots * n_planes * bt * key * 4 + 8 * 1024 * 1024)
    body = functools.partial(_gla_body, layer=layer, nb=nb, t_rows=t_rows, bt=bt,
                             n_heads=n_heads, hk=hk, hv=hv, unrolled=unrolled)
    if unrolled:
        proj = proj.reshape(n_seq, t, proj.shape[1])
        row_block = lambda width: pl.BlockSpec((nb, t_rows, width), lambda b, c: (b, c, 0))
        og_shape = (n_seq, t, d)
    else:
        row_block = lambda width: pl.BlockSpec((rows, width), lambda b, c: (b, 0))
        og_shape = (n_seq * t, d)
    state_spec = pl.BlockSpec((None, nb, n_heads, hk, hv), lambda b, c: (layer, b, 0, 0, 0))
    aliased = () if snew is None else (snew,)
    og, snew = pl.pallas_call(
        body,
        grid=(n_seq // nb, nc),
        in_specs=[row_block(proj.shape[-1]),
                  state_spec, _resident(lb_logits.shape), _layer_of(gn, layer),
                  _resident(scan.shape), _resident(lvl.shape)]
                 + [pl.BlockSpec(memory_space=pl.ANY)] * len(aliased),
        out_specs=[row_block(d), state_spec],
        out_shape=[jax.ShapeDtypeStruct(og_shape, BF16), jax.ShapeDtypeStruct(s0.shape, F32)],
        input_output_aliases={6: 1} if aliased else {},
        scratch_shapes=[pltpu.VMEM((nb, n_heads, hv, hk), F32) if unrolled
                        else pltpu.VMEM((n_slots, 2 * bt, key), BF16),
                        pltpu.VMEM((rows, d), F32),
                        pltpu.VMEM((n_slots, n_planes, bt, key), F32)],
        compiler_params=pltpu.CompilerParams(
            dimension_semantics=("parallel", "arbitrary"), vmem_limit_bytes=_vmem_limit(est)),
        name="hgrn_gla",
    )(proj, s0, lb_logits, gn, scan, lvl, *aliased)
    return og.reshape(n_seq * t, d), snew


def _conv_body(x_ref, g_ref, w_ref, wc_ref, buf_ref, a_ref, nbuf_ref, *, nb, tt):
    c = pl.program_id(1)
    d = x_ref.shape[1]
    rows = nb * tt

    @pl.when(c == 0)
    def _():
        nbuf_ref[...] = buf_ref[...]

    h = _rmsnorm(x_ref[...], g_ref[...]).astype(BF16)
    t = lax.broadcasted_iota(jnp.int32, (nb, tt, CONV_COLS), 1)

    def project(n):
        cols = [slice(k * d + n * CONV_COLS, k * d + (n + 1) * CONV_COLS) for k in range(3)]
        return tuple(_dot(h, w_ref[:, s]) for s in cols)

    def tail(n, gate_b, gate_c, third):
        cols = slice(n * CONV_COLS, (n + 1) * CONV_COLS)
        u = (gate_c * third).reshape(nb, tt, CONV_COLS)
        p2, p1 = nbuf_ref[:, 0:1, cols], nbuf_ref[:, 1:2, cols]
        u1 = jnp.where(t == 0, p1, pltpu.roll(u, 1, axis=1))
        u2 = jnp.where(t == 0, p2, jnp.where(t == 1, p1, pltpu.roll(u, 2, axis=1)))
        w0, w1, w2 = (wc_ref[k:k + 1, cols].reshape(1, 1, CONV_COLS) for k in range(3))
        y = w0 * u2 + w1 * u1 + w2 * u
        a_ref[:, cols] = (gate_b * y.reshape(rows, CONV_COLS)).astype(BF16)
        nbuf_ref[:, :, cols] = u[:, tt - 2:tt, :]

    pending = None
    for n in range(d // CONV_COLS):
        current = (n,) + project(n)
        if pending is not None:
            tail(*pending)
        pending = current
    tail(*pending)


def _conv_mixer(x, g, w_in, w_conv, buf, *, l, j, n_seq, t):
    r, d = x.shape
    if t >= CONV_ROWS:
        nb, tt = 1, CONV_ROWS
    else:
        nb, tt = min(CONV_ROWS // t, n_seq), t
        if nb == n_seq and nb * t >= 2 * MIN_PIPELINED_ROWS and nb % 2 == 0:
            nb //= 2
    nc = t // tt
    rows = nb * tt
    assert t % tt == 0 and n_seq % nb == 0 and (nb == 1 or nc == 1) and tt % 8 == 0 and tt >= CONV_W - 1
    est = 2 * rows * d * 4 + d * 3 * d * 2 + 2 * rows * d * 2 + 10 * rows * d * 4
    buf_spec = pl.BlockSpec((None, nb, CONV_W - 1, d), lambda b, c: (j, b, 0, 0))
    nbuf_spec = pl.BlockSpec((nb, CONV_W - 1, d), lambda b, c: (b, 0, 0))
    return pl.pallas_call(
        functools.partial(_conv_body, nb=nb, tt=tt),
        grid=(n_seq // nb, nc),
        in_specs=[pl.BlockSpec((rows, d), lambda b, c: (b * nc + c, 0)), _layer_of(g, l),
                  _layer_of(w_in, j), _layer_of(w_conv, j), buf_spec],
        out_specs=[pl.BlockSpec((rows, d), lambda b, c: (b * nc + c, 0)), nbuf_spec],
        out_shape=[jax.ShapeDtypeStruct((r, d), BF16), jax.ShapeDtypeStruct(buf.shape[1:], F32)],
        compiler_params=pltpu.CompilerParams(
            dimension_semantics=("parallel", "arbitrary"), vmem_limit_bytes=_vmem_limit(est)),
        name="conv_mixer",
    )(x, g, w_in, w_conv, buf)


def _ff_chunks(dff):
    tiles = dff // V7X_MXU_WIDTH
    assert tiles * V7X_MXU_WIDTH == dff
    bounds = [round(n * tiles / FF_CHUNKS) * V7X_MXU_WIDTH for n in range(FF_CHUNKS + 1)]
    return tuple(zip(bounds[:-1], bounds[1:]))


def _out_ffn_body(x_ref, a_ref, wo_ref, g_ref, wg_ref, wu_ref, wd_ref, gf_ref, y_ref, *,
                  ff_chunks, final):
    x = x_ref[...] + _dot(a_ref[...], wo_ref[...])
    h = _rmsnorm(x, g_ref[...]).astype(BF16)
    for lo, hi in ff_chunks:
        act = _silu(_dot(h, wg_ref[:, lo:hi])) * _dot(h, wu_ref[:, lo:hi])
        x = x + _dot(act.astype(BF16), wd_ref[lo:hi, :])
    y_ref[...] = _rmsnorm(x, gf_ref[...]) if final else x


def _out_ffn(x, a, w_out, g, w_gate, w_up, w_down, g_final, *, l, j, tm, final):
    r, d = x.shape
    dff = w_gate.shape[2]
    ff_chunks = _ff_chunks(dff)
    widest = max(hi - lo for lo, hi in ff_chunks)
    weights = (d * d + 3 * d * dff) * 2
    est = 2 * tm * d * (4 + 2 + 4) + weights + 3 * tm * d * 4 + tm * widest * 14
    row = lambda i: (i, 0)
    return pl.pallas_call(
        functools.partial(_out_ffn_body, ff_chunks=ff_chunks, final=final),
        grid=(r // tm,),
        in_specs=[pl.BlockSpec((tm, d), row), pl.BlockSpec((tm, d), row), _layer_of(w_out, j),
                  _layer_of(g, l), _layer_of(w_gate, l), _layer_of(w_up, l), _layer_of(w_down, l),
                  _resident(g_final.shape)],
        out_specs=pl.BlockSpec((tm, d), row),
        out_shape=jax.ShapeDtypeStruct((r, d), F32),
        compiler_params=pltpu.CompilerParams(
            dimension_semantics=("parallel",), vmem_limit_bytes=_vmem_limit(est)),
        name="out_ffn",
    )(x, a, w_out, g, w_gate, w_up, w_down, g_final)


def _trunk(x, s_hgrn, s_conv, p, *, n_seq, t):
    depth = p["norm_mix"].shape[0]
    rows = x.shape[0]
    tm = min(ROW_TILE, rows // 2) if rows >= 2 * MIN_PIPELINED_ROWS else rows
    assert rows % tm == 0
    new_h, new_c = None, []
    for l in range(depth):
        j = l // 2
        if l % 2 == 0:
            proj = _norm_matmul(x, p["norm_mix"], p["hgrn_w_in"], l=l, j=j, tm=tm)
            a, new_h = _gla(proj, s_hgrn, p["hgrn_lb_logits"], p["hgrn_norm"], new_h,
                            layer=j, n_seq=n_seq, t=t)
            w_out = p["hgrn_w_out"]
        else:
            a, s = _conv_mixer(x, p["norm_mix"], p["conv_w_in"], p["conv_w"], s_conv,
                               l=l, j=j, n_seq=n_seq, t=t)
            new_c.append(s)
            w_out = p["conv_w_out"]
        x = _out_ffn(x, a, w_out, p["norm_ffn"], p["ffn_w_gate"], p["ffn_w_up"], p["ffn_w_down"],
                     p["norm_final"], l=l, j=j, tm=tm, final=(l == depth - 1))
    return x, new_h, jnp.stack(new_c)


def kernel(x_prompt, x_sample, state_hgrn, state_conv, meta_tokens, norm_mix, norm_ffn, norm_final,
           hgrn_w_in, hgrn_w_out, hgrn_lb_logits, hgrn_norm, conv_w_in, conv_w, conv_w_out,
           ffn_w_gate, ffn_w_up, ffn_w_down):
    bsz, seq, d = x_prompt.shape
    dec_b, dec_t, _ = x_sample.shape
    n_meta = meta_tokens.shape[0]
    p = dict(norm_mix=norm_mix[:, None], norm_ffn=norm_ffn[:, None], norm_final=norm_final[None],
             hgrn_lb_logits=hgrn_lb_logits, hgrn_norm=hgrn_norm[:, None], conv_w=conv_w,
             hgrn_w_in=hgrn_w_in.astype(BF16), hgrn_w_out=hgrn_w_out.astype(BF16),
             conv_w_in=conv_w_in.astype(BF16), conv_w_out=conv_w_out.astype(BF16),
             ffn_w_gate=ffn_w_gate.astype(BF16), ffn_w_up=ffn_w_up.astype(BF16),
             ffn_w_down=ffn_w_down.astype(BF16))

    zeros_h = jnp.zeros((state_hgrn.shape[0], 1) + state_hgrn.shape[2:], F32)
    zeros_c = jnp.zeros((state_conv.shape[0], 1) + state_conv.shape[2:], F32)
    _, meta_h, meta_c = _trunk(meta_tokens.astype(F32), zeros_h, zeros_c, p, n_seq=1, t=n_meta)

    prompt_h0 = jnp.broadcast_to(meta_h, (meta_h.shape[0], bsz) + meta_h.shape[2:])
    prompt_c0 = jnp.broadcast_to(meta_c, (meta_c.shape[0], bsz) + meta_c.shape[2:])
    y_prompt, new_hgrn_prompt, new_conv_prompt = _trunk(
        x_prompt.reshape(bsz * seq, d), prompt_h0, prompt_c0, p, n_seq=bsz, t=seq)

    y_sample, new_hgrn_sample, new_conv_sample = _trunk(
        x_sample.reshape(dec_b * dec_t, d), state_hgrn, state_conv, p, n_seq=dec_b, t=dec_t)

    return (y_prompt.reshape(bsz, seq, d), y_sample.reshape(dec_b, dec_t, d),
            new_hgrn_prompt, new_conv_prompt, new_hgrn_sample, new_conv_sample)
```

```python
import functools

import jax
import jax.numpy as jnp
import numpy as np
from jax import lax
from jax.experimental import pallas as pl
from jax.experimental.pallas import tpu as pltpu

F32, BF16 = jnp.float32, jnp.bfloat16
EPS = 1e-6
CONV_W = 3

V7X_VMEM_BYTES = 64 * 1024 * 1024
V7X_VMEM_COMPILER_RESERVE = 6 * 1024 * 1024
BF16_SUBLANES = 16
LANES = 128
V7X_MXU_WIDTH = 256
SCAN_LEVELS = (2, 4)

ROW_TILE = 1024
MIN_PIPELINED_ROWS = 256
GLA_CHUNK = 64
GLA_LONG_SEQS_PER_STEP = 8
GLA_SEQS_PER_STEP = 16
GLA_SHORT_SEQS_INTERLEAVED = 4
CONV_ROWS = 1024
CONV_COLS = 256
FF_CHUNKS = 4


def _vmem_limit(est_bytes):
    return int(min(max(est_bytes, 16 * 1024 * 1024), V7X_VMEM_BYTES - V7X_VMEM_COMPILER_RESERVE))


def _rmsnorm(x, g):
    return x * lax.rsqrt(jnp.mean(x * x, axis=-1, keepdims=True) + EPS) * g


def _silu(x):
    half = 0.5 * x
    return half + half * jnp.tanh(half)


def _dot(a, b):
    return jnp.dot(a, b, preferred_element_type=F32)


def _dot_nt(a, b):
    return lax.dot_general(a, b, (((1,), (1,)), ((), ())), preferred_element_type=F32)


def _dot_tn(a, b):
    return lax.dot_general(a, b, (((0,), (0,)), ((), ())), preferred_element_type=F32)


def _resident(shape):
    zeros = (0,) * len(shape)
    return pl.BlockSpec(shape, lambda *_: zeros, pipeline_mode=pl.Buffered(1))


def _layer_of(stacked, l):
    index = (l,) + (0,) * (stacked.ndim - 1)
    return pl.BlockSpec((None,) + stacked.shape[1:], lambda *_: index, pipeline_mode=pl.Buffered(1))


def _norm_matmul_body(x_ref, g_ref, w_ref, o_ref, *, n_chunk):
    h = _rmsnorm(x_ref[...], g_ref[...]).astype(BF16)
    for c in range(0, o_ref.shape[1], n_chunk):
        o_ref[:, c:c + n_chunk] = _dot(h, w_ref[:, c:c + n_chunk])


def _norm_matmul(x, g, w, *, l, j, tm):
    r, d = x.shape
    n = w.shape[2]
    est = 2 * tm * d * 4 + d * n * 2 + 2 * tm * n * 4 + tm * d * 8 + tm * 1024 * 8
    return pl.pallas_call(
        functools.partial(_norm_matmul_body, n_chunk=1024),
        grid=(r // tm,),
        in_specs=[pl.BlockSpec((tm, d), lambda i: (i, 0)), _layer_of(g, l), _layer_of(w, j)],
        out_specs=pl.BlockSpec((tm, n), lambda i: (i, 0)),
        out_shape=jax.ShapeDtypeStruct((r, n), F32),
        compiler_params=pltpu.CompilerParams(
            dimension_semantics=("parallel",), vmem_limit_bytes=_vmem_limit(est)),
        name="hgrn_in_proj",
    )(x, g, w)


def _gla_constants(bt, stack):
    i = np.arange(bt)[:, None]
    t = np.arange(bt)[None, :]
    blocks = [t <= i]
    for s in SCAN_LEVELS:
        mid = (i // (2 * s)) * 2 * s + s - 1
        blocks.append(np.where(i <= mid, (t > i) & (t <= mid), (t > mid) & (t <= i)))
    scan = np.concatenate(blocks, axis=0).astype(np.float32)
    r = np.arange(stack)[:, None]
    c = np.arange(LANES)[None, :] + (r // LANES) * LANES
    x = (r % bt) ^ (c % bt)
    lvl = np.where(x == 0, 0, 1 + np.floor(np.log2(np.maximum(x, 1))).astype(np.int32))
    lvl = np.where((r // bt == c // bt) & (r >= c), lvl, -1).astype(np.int32)
    return jnp.asarray(np.concatenate([scan, scan], axis=1), BF16), jnp.asarray(lvl)


def _gla_body(proj_ref, s0_ref, lbl_ref, gn_ref, scan_ref, lvl_ref, *rest,
              layer, nb, t_rows, bt, n_heads, hk, hv, unrolled):
    og_ref, snew_ref, aux_ref, o_ref, w_ref = rest[-5:]
    st_ref = aux_ref if unrolled else None
    hl_ref = None if unrolled else aux_ref
    n_slots = w_ref.shape[0]
    c = pl.program_id(1)
    key = n_heads * hk
    d = n_heads * hv
    pad = bt - t_rows
    stack = lvl_ref.shape[0]
    group = stack // bt
    width = group * hk
    tiles = stack // LANES

    def proj_rows(i, lo, hi):
        if unrolled:
            return proj_ref[i, :, lo:hi]
        return proj_ref[pl.ds(pl.multiple_of(i * t_rows, t_rows), t_rows), lo:hi]

    if unrolled:
        @pl.when(c == 0)
        def _():
            for i in range(nb):
                for h in range(n_heads):
                    st_ref[i, h] = s0_ref[i, h].T

    logits = lbl_ref[...]
    ex = jnp.exp(logits - jnp.max(logits, axis=0, keepdims=True))
    sm = ex / jnp.sum(ex, axis=0, keepdims=True)
    lb = jnp.sum(sm[0:layer + 1], axis=0, keepdims=True) - sm[0:1]

    om = 1.0 - lb

    sub = lax.broadcasted_iota(jnp.int32, (1, 8, width), 1)
    lvl = lvl_ref[...]

    def padded(x):
        return x if pad == 0 else jnp.concatenate([x, jnp.zeros((pad, x.shape[1]), x.dtype)], axis=0)

    def by_role(s, second_half, first_half):
        shape = second_half.shape
        pick = jnp.where((sub & s) != 0, second_half.reshape(bt // 8, 8, width),
                         first_half.reshape(bt // 8, 8, width))
        return pick.reshape(shape)

    def stacked(x):
        return jnp.concatenate([x[:, e * hk:(e + 1) * hk] for e in range(group)], axis=0).astype(BF16)

    def gram(x, y):
        full = _dot_nt(x, y)
        return jnp.concatenate(
            [full[n * LANES:(n + 1) * LANES, n * LANES:(n + 1) * LANES] for n in range(tiles)], axis=0)

    n_stacks = n_heads // group

    def prepare(i, slot):
        wv = w_ref.at[slot]
        q = padded(proj_rows(i, 0, key))
        z = padded(proj_rows(i, key, 2 * key))
        w = om * jax.nn.sigmoid(z)
        f = lb + w
        lf = jnp.log(f)
        k = om - w
        if pad:
            real = lax.broadcasted_iota(jnp.int32, (bt, key), 0) < t_rows
            f = jnp.where(real, f, 1.0)
            lf = jnp.where(real, lf, 0.0)
            k = jnp.where(real, k, 0.0)
        hi = lf.astype(BF16)
        hl = jnp.concatenate([hi, (lf - hi.astype(F32)).astype(BF16)], axis=0)
        if not unrolled:
            hl_ref[slot] = hl
        wv[0] = _silu(q)
        wv[1] = k
        wv[2] = f
        wv[3:3 + 1 + len(SCAN_LEVELS)] = _dot(scan_ref[...], hl).reshape(1 + len(SCAN_LEVELS), bt, key)

    def intra(i, slot, p):
        wv = w_ref.at[slot]
        lanes = slice(p * width, (p + 1) * width)
        qs, k, f, b = wv[0, :, lanes], wv[1, :, lanes], wv[2, :, lanes], wv[3, :, lanes]
        v = stacked(padded(proj_rows(i, 2 * key + p * width, 2 * key + (p + 1) * width)))

        operands = [by_role(1, qs * f, k)]
        for n, s in enumerate(SCAN_LEVELS):
            operands.append(by_role(s, qs, k) * jnp.exp(wv[4 + n, :, lanes]))
        s = 2 * SCAN_LEVELS[-1]
        while s < bt:
            pieces = []
            for m in range(0, bt, 2 * s):
                mid = b[m + s - 1:m + s]
                pieces.append(k[m:m + s] * jnp.exp(mid - b[m:m + s]))
                pieces.append(qs[m + s:m + 2 * s] * jnp.exp(b[m + s:m + 2 * s] - mid))
            operands.append(jnp.concatenate(pieces, axis=0))
            s *= 2

        a = jnp.where(lvl == 0, gram(stacked(qs), stacked(k)), 0.0)
        for n, x in enumerate(operands):
            x = stacked(x)
            a = jnp.where(lvl == n + 1, gram(x, x), a)
        bl = b[bt - 1:bt]
        return (a.astype(BF16), v, (qs * jnp.exp(b)).astype(BF16), (k * jnp.exp(bl - b)).astype(BF16),
                jnp.exp(bl))

    def finish(i, p, a, v, qb, kt, decay):
        rows = pl.ds(i * t_rows if unrolled else pl.multiple_of(i * t_rows, t_rows), t_rows)
        o_intra = jnp.concatenate(
            [_dot(a[n * LANES:(n + 1) * LANES], v[n * LANES:(n + 1) * LANES]) for n in range(tiles)],
            axis=0)
        for e in range(group):
            h = p * group + e
            head = slice(e * hk, (e + 1) * hk)
            st = st_ref[i, h]
            o = o_intra[e * bt:(e + 1) * bt] + _dot_nt(qb[:, head], st.astype(BF16))
            o_ref[rows, h * hv:(h + 1) * hv] = o[0:t_rows]
            st_ref[i, h] = st * decay[:, head] + _dot_tn(v[e * bt:(e + 1) * bt], kt[:, head])

    def finish_in_place(i, slot, a, v, qb, kt):
        rows = pl.ds(pl.multiple_of(i * t_rows, t_rows), t_rows)
        o_intra = _dot(a, v)
        sum_rows = jnp.concatenate([jnp.ones((2 * bt, hv), BF16), jnp.zeros((2 * bt, hv), BF16)], axis=1)
        for h in range(n_heads):
            head = slice(h * hk, (h + 1) * hk)
            s = s0_ref[i, h]
            o = o_intra[h * bt:(h + 1) * bt] + _dot(qb[:, head], s.astype(BF16))
            o_ref[rows, h * hv:(h + 1) * hv] = o[0:t_rows]
            lhs = jnp.concatenate([hl_ref[slot, :, head], kt[:, head]], axis=0)
            rhs = jnp.concatenate(
                [sum_rows,
                 jnp.concatenate([jnp.zeros((bt, hv), BF16), v[h * bt:(h + 1) * bt]], axis=1)], axis=0)
            upd = _dot_tn(lhs, rhs)
            snew_ref[i, h] = jnp.exp(upd[:, 0:hv]) * s + upd[:, hv:2 * hv]

    def interleaved_seqs(j):
        seqs = [j * n_slots + slot for slot in range(n_slots)]
        for slot, i in enumerate(seqs):
            prepare(i, slot)
        parts = [intra(i, slot, 0)[:4] for slot, i in enumerate(seqs)]
        for slot, i in enumerate(seqs):
            finish_in_place(i, slot, *parts[slot])

    if unrolled:
        prepare(0, 0)
        pending = None
        for i in range(nb):
            for p in range(n_stacks):
                if p == 0 and i + 1 < nb:
                    prepare(i + 1, i + 1)
                current = (i, p) + intra(i, i, p)
                if pending is not None:
                    finish(*pending)
                pending = current
        finish(*pending)
    else:
        lax.fori_loop(0, nb // n_slots, lambda j, _: (interleaved_seqs(j), 0)[1], 0)

    gate_lanes = slice(2 * key + d, 2 * key + 2 * d)
    gate = proj_ref[:, :, gate_lanes].reshape(nb * t_rows, d) if unrolled else proj_ref[:, gate_lanes]
    og = (_rmsnorm(o_ref[...], gn_ref[...]) * _silu(gate)).astype(BF16)
    og_ref[...] = og.reshape(og_ref.shape)

    if unrolled:
        @pl.when(c == pl.num_programs(1) - 1)
        def _():
            for i in range(nb):
                for h in range(n_heads):
                    snew_ref[i, h] = st_ref[i, h].T


def _gla(proj, s0, lb_logits, gn, snew, *, layer, n_seq, t):
    _, _, n_heads, hk, hv = s0.shape
    d = n_heads * hv
    key = n_heads * hk
    unrolled = t >= GLA_CHUNK
    if unrolled:
        nb, t_rows = min(GLA_LONG_SEQS_PER_STEP, n_seq), GLA_CHUNK
    else:
        nb, t_rows = min(GLA_SEQS_PER_STEP, n_seq), t
    bt = max(t_rows, BF16_SUBLANES)
    nc = t // t_rows
    rows = nb * t_rows
    assert t % t_rows == 0 and n_seq % nb == 0 and (unrolled or nc == 1) and rows % BF16_SUBLANES == 0
    stack = min(V7X_MXU_WIDTH, n_heads * bt)
    assert LANES % bt == 0 and stack % LANES == 0 and (n_heads * bt) % stack == 0 and hk == LANES
    assert bt > 2 * SCAN_LEVELS[-1] and (unrolled or stack == n_heads * bt == LANES)
    scan, lvl = _gla_constants(bt, stack)
    n_planes = 4 + len(SCAN_LEVELS)
    n_slots = nb if unrolled else (GLA_SHORT_SEQS_INTERLEAVED if nb % GLA_SHORT_SEQS_INTERLEAVED == 0 else 1)
    state_bytes = nb * n_heads * hk * hv * 4
    est = (2 * rows * proj.shape[1] * 4 + 5 * state_bytes + 4 * rows * d * 4
           + 3 * n_slots * n_planes * bt * key * 4 + 8 * 1024 * 1024)
    body = functools.partial(_gla_body, layer=layer, nb=nb, t_rows=t_rows, bt=bt,
                             n_heads=n_heads, hk=hk, hv=hv, unrolled=unrolled)
    if unrolled:
        proj = proj.reshape(n_seq, t, proj.shape[1])
        row_block = lambda width: pl.BlockSpec((nb, t_rows, width), lambda b, c: (b, c, 0))
        og_shape = (n_seq, t, d)
    else:
        row_block = lambda width: pl.BlockSpec((rows, width), lambda b, c: (b, 0))
        og_shape = (n_seq * t, d)
    state_spec = pl.BlockSpec((None, nb, n_heads, hk, hv), lambda b, c: (layer, b, 0, 0, 0))
    aliased = () if snew is None else (snew,)
    og, snew = pl.pallas_call(
        body,
        grid=(n_seq // nb, nc),
        in_specs=[row_block(proj.shape[-1]),
                  state_spec, _resident(lb_logits.shape), _layer_of(gn, layer),
                  _resident(scan.shape), _resident(lvl.shape)]
                 + [pl.BlockSpec(memory_space=pl.ANY)] * len(aliased),
        out_specs=[row_block(d), state_spec],
        out_shape=[jax.ShapeDtypeStruct(og_shape, BF16), jax.ShapeDtypeStruct(s0.shape, F32)],
        input_output_aliases={6: 1} if aliased else {},
        scratch_shapes=[pltpu.VMEM((nb, n_heads, hv, hk), F32) if unrolled
                        else pltpu.VMEM((n_slots, 2 * bt, key), BF16),
                        pltpu.VMEM((rows, d), F32),
                        pltpu.VMEM((n_slots, n_planes, bt, key), F32)],
        compiler_params=pltpu.CompilerParams(
            dimension_semantics=("parallel", "arbitrary"), vmem_limit_bytes=_vmem_limit(est)),
        name="hgrn_gla",
    )(proj, s0, lb_logits, gn, scan, lvl, *aliased)
    return og.reshape(n_seq * t, d), snew


def _conv_body(x_ref, g_ref, w_ref, wc_ref, buf_ref, a_ref, nbuf_ref, *, nb, tt):
    c = pl.program_id(1)
    d = x_ref.shape[1]
    rows = nb * tt

    @pl.when(c == 0)
    def _():
        nbuf_ref[...] = buf_ref[...]

    h = _rmsnorm(x_ref[...], g_ref[...]).astype(BF16)
    t = lax.broadcasted_iota(jnp.int32, (nb, tt, CONV_COLS), 1)

    def project(n):
        cols = [slice(k * d + n * CONV_COLS, k * d + (n + 1) * CONV_COLS) for k in range(3)]
        return tuple(_dot(h, w_ref[:, s]) for s in cols)

    def tail(n, gate_b, gate_c, third):
        cols = slice(n * CONV_COLS, (n + 1) * CONV_COLS)
        u = (gate_c * third).reshape(nb, tt, CONV_COLS)
        p2, p1 = nbuf_ref[:, 0:1, cols], nbuf_ref[:, 1:2, cols]
        u1 = jnp.where(t == 0, p1, pltpu.roll(u, 1, axis=1))
        u2 = jnp.where(t == 0, p2, jnp.where(t == 1, p1, pltpu.roll(u, 2, axis=1)))
        w0, w1, w2 = (wc_ref[k:k + 1, cols].reshape(1, 1, CONV_COLS) for k in range(3))
        y = w0 * u2 + w1 * u1 + w2 * u
        a_ref[:, cols] = (gate_b * y.reshape(rows, CONV_COLS)).astype(BF16)
        nbuf_ref[:, :, cols] = u[:, tt - 2:tt, :]

    pending = None
    for n in range(d // CONV_COLS):
        current = (n,) + project(n)
        if pending is not None:
            tail(*pending)
        pending = current
    tail(*pending)


def _conv_mixer(x, g, w_in, w_conv, buf, *, l, j, n_seq, t):
    r, d = x.shape
    if t >= CONV_ROWS:
        nb, tt = 1, CONV_ROWS
    else:
        nb, tt = min(CONV_ROWS // t, n_seq), t
        if nb == n_seq and nb * t >= 2 * MIN_PIPELINED_ROWS and nb % 2 == 0:
            nb //= 2
    nc = t // tt
    rows = nb * tt
    assert t % tt == 0 and n_seq % nb == 0 and (nb == 1 or nc == 1) and tt % 8 == 0 and tt >= CONV_W - 1
    est = 2 * rows * d * 4 + d * 3 * d * 2 + 2 * rows * d * 2 + 10 * rows * d * 4
    buf_spec = pl.BlockSpec((None, nb, CONV_W - 1, d), lambda b, c: (j, b, 0, 0))
    nbuf_spec = pl.BlockSpec((nb, CONV_W - 1, d), lambda b, c: (b, 0, 0))
    return pl.pallas_call(
        functools.partial(_conv_body, nb=nb, tt=tt),
        grid=(n_seq // nb, nc),
        in_specs=[pl.BlockSpec((rows, d), lambda b, c: (b * nc + c, 0)), _layer_of(g, l),
                  _layer_of(w_in, j), _layer_of(w_conv, j), buf_spec],
        out_specs=[pl.BlockSpec((rows, d), lambda b, c: (b * nc + c, 0)), nbuf_spec],
        out_shape=[jax.ShapeDtypeStruct((r, d), BF16), jax.ShapeDtypeStruct(buf.shape[1:], F32)],
        compiler_params=pltpu.CompilerParams(
            dimension_semantics=("parallel", "arbitrary"), vmem_limit_bytes=_vmem_limit(est)),
        name="conv_mixer",
    )(x, g, w_in, w_conv, buf)


def _ff_chunks(dff):
    tiles = dff // V7X_MXU_WIDTH
    assert tiles * V7X_MXU_WIDTH == dff
    bounds = [round(n * tiles / FF_CHUNKS) * V7X_MXU_WIDTH for n in range(FF_CHUNKS + 1)]
    return tuple(zip(bounds[:-1], bounds[1:]))


def _out_ffn_body(x_ref, a_ref, wo_ref, g_ref, wg_ref, wu_ref, wd_ref, gf_ref, y_ref, *,
                  ff_chunks, final):
    x = x_ref[...] + _dot(a_ref[...], wo_ref[...])
    h = _rmsnorm(x, g_ref[...]).astype(BF16)
    for lo, hi in ff_chunks:
        act = _silu(_dot(h, wg_ref[:, lo:hi])) * _dot(h, wu_ref[:, lo:hi])
        x = x + _dot(act.astype(BF16), wd_ref[lo:hi, :])
    y_ref[...] = _rmsnorm(x, gf_ref[...]) if final else x


def _out_ffn(x, a, w_out, g, w_gate, w_up, w_down, g_final, *, l, j, tm, final):
    r, d = x.shape
    dff = w_gate.shape[1]
    ff_chunks = _ff_chunks(dff)
    widest = max(hi - lo for lo, hi in ff_chunks)
    weights = (d * d + 3 * d * dff) * 2
    est = 2 * tm * d * (4 + 2 + 4) + weights + 3 * tm * d * 4 + tm * widest * 14
    row = lambda i: (i, 0)
    return pl.pallas_call(
        functools.partial(_out_ffn_body, ff_chunks=ff_chunks, final=final),
        grid=(r // tm,),
        in_specs=[pl.BlockSpec((tm, d), row), pl.BlockSpec((tm, d), row), _layer_of(w_out, j),
                  _layer_of(g, l), _resident(w_gate.shape), _resident(w_up.shape), _resident(w_down.shape),
                  _resident(g_final.shape)],
        out_specs=pl.BlockSpec((tm, d), row),
        out_shape=jax.ShapeDtypeStruct((r, d), F32),
        compiler_params=pltpu.CompilerParams(
            dimension_semantics=("parallel",), vmem_limit_bytes=_vmem_limit(est)),
        name="out_ffn",
    )(x, a, w_out, g, w_gate, w_up, w_down, g_final)


def _out_ffn_cast_body(x_ref, a_ref, wo_ref, g_ref, wg_ref, wu_ref, wd_ref, gf_ref,
                       y_ref, wgb_ref, wub_ref, wdb_ref, h_ref, acc_ref, *, final):
    c = pl.program_id(0)

    @pl.when(c == 0)
    def _():
        x = x_ref[...] + _dot(a_ref[...], wo_ref[...])
        h_ref[...] = _rmsnorm(x, g_ref[...]).astype(BF16)
        acc_ref[...] = x

    wg, wu, wd = wg_ref[...].astype(BF16), wu_ref[...].astype(BF16), wd_ref[...].astype(BF16)
    wgb_ref[...], wub_ref[...], wdb_ref[...] = wg, wu, wd
    h = h_ref[...]
    acc_ref[...] += _dot((_silu(_dot(h, wg)) * _dot(h, wu)).astype(BF16), wd)

    @pl.when(c == pl.num_programs(0) - 1)
    def _():
        y_ref[...] = _rmsnorm(acc_ref[...], gf_ref[...]) if final else acc_ref[...]


def _out_ffn_cast(x, a, w_out, g, w_gate, w_up, w_down, g_final, *, l, j, final):
    r, d = x.shape
    dff = w_gate.shape[2]
    fc = V7X_MXU_WIDTH
    assert dff % fc == 0
    est = 4 * r * d * 4 + d * d * 2 + 2 * 3 * d * fc * (4 + 2) + 3 * d * fc * 2 + 8 * r * fc * 4
    whole = lambda shape: pl.BlockSpec(shape, lambda c: (0,) * len(shape))
    y, wg, wu, wd = pl.pallas_call(
        functools.partial(_out_ffn_cast_body, final=final),
        grid=(dff // fc,),
        in_specs=[whole((r, d)), whole((r, d)), _layer_of(w_out, j), _layer_of(g, l),
                  pl.BlockSpec((None, d, fc), lambda c: (l, 0, c)),
                  pl.BlockSpec((None, d, fc), lambda c: (l, 0, c)),
                  pl.BlockSpec((None, fc, d), lambda c: (l, c, 0)),
                  _resident(g_final.shape)],
        out_specs=[whole((r, d)), pl.BlockSpec((d, fc), lambda c: (0, c)),
                   pl.BlockSpec((d, fc), lambda c: (0, c)), pl.BlockSpec((fc, d), lambda c: (c, 0))],
        out_shape=[jax.ShapeDtypeStruct((r, d), F32), jax.ShapeDtypeStruct((d, dff), BF16),
                   jax.ShapeDtypeStruct((d, dff), BF16), jax.ShapeDtypeStruct((dff, d), BF16)],
        scratch_shapes=[pltpu.VMEM((r, d), BF16), pltpu.VMEM((r, d), F32)],
        compiler_params=pltpu.CompilerParams(
            dimension_semantics=("arbitrary",), vmem_limit_bytes=_vmem_limit(est)),
        name="out_ffn_cast",
    )(x, a, w_out, g, w_gate, w_up, w_down, g_final)
    return y, (wg, wu, wd)


def _trunk(x, s_hgrn, s_conv, p, ffn_bf16, *, n_seq, t):
    depth = p["norm_mix"].shape[0]
    rows = x.shape[0]
    tm = min(ROW_TILE, rows // 2) if rows >= 2 * MIN_PIPELINED_ROWS else rows
    assert rows % tm == 0
    new_h, new_c = None, []
    for l in range(depth):
        j = l // 2
        if l % 2 == 0:
            proj = _norm_matmul(x, p["norm_mix"], p["hgrn_w_in"], l=l, j=j, tm=tm)
            a, new_h = _gla(proj, s_hgrn, p["hgrn_lb_logits"], p["hgrn_norm"], new_h,
                            layer=j, n_seq=n_seq, t=t)
            w_out = p["hgrn_w_out"]
        else:
            a, s = _conv_mixer(x, p["norm_mix"], p["conv_w_in"], p["conv_w"], s_conv,
                               l=l, j=j, n_seq=n_seq, t=t)
            new_c.append(s)
            w_out = p["conv_w_out"]
        if len(ffn_bf16) <= l:
            x, cast = _out_ffn_cast(x, a, w_out, p["norm_ffn"], p["ffn_w_gate"], p["ffn_w_up"],
                                    p["ffn_w_down"], p["norm_final"], l=l, j=j, final=(l == depth - 1))
            ffn_bf16.append(cast)
        else:
            x = _out_ffn(x, a, w_out, p["norm_ffn"], *ffn_bf16[l], p["norm_final"],
                         l=l, j=j, tm=tm, final=(l == depth - 1))
    return x, new_h, jnp.stack(new_c)


def kernel(x_prompt, x_sample, state_hgrn, state_conv, meta_tokens, norm_mix, norm_ffn, norm_final,
           hgrn_w_in, hgrn_w_out, hgrn_lb_logits, hgrn_norm, conv_w_in, conv_w, conv_w_out,
           ffn_w_gate, ffn_w_up, ffn_w_down):
    bsz, seq, d = x_prompt.shape
    dec_b, dec_t, _ = x_sample.shape
    n_meta = meta_tokens.shape[0]
    p = dict(norm_mix=norm_mix[:, None], norm_ffn=norm_ffn[:, None], norm_final=norm_final[None],
             hgrn_lb_logits=hgrn_lb_logits, hgrn_norm=hgrn_norm[:, None], conv_w=conv_w,
             hgrn_w_in=hgrn_w_in.astype(BF16), hgrn_w_out=hgrn_w_out.astype(BF16),
             conv_w_in=conv_w_in.astype(BF16), conv_w_out=conv_w_out.astype(BF16),
             ffn_w_gate=ffn_w_gate, ffn_w_up=ffn_w_up, ffn_w_down=ffn_w_down)
    ffn_bf16 = []

    zeros_h = jnp.zeros((state_hgrn.shape[0], 1) + state_hgrn.shape[2:], F32)
    zeros_c = jnp.zeros((state_conv.shape[0], 1) + state_conv.shape[2:], F32)
    _, meta_h, meta_c = _trunk(meta_tokens.astype(F32), zeros_h, zeros_c, p, ffn_bf16, n_seq=1, t=n_meta)

    prompt_h0 = jnp.broadcast_to(meta_h, (meta_h.shape[0], bsz) + meta_h.shape[2:])
    prompt_c0 = jnp.broadcast_to(meta_c, (meta_c.shape[0], bsz) + meta_c.shape[2:])
    y_prompt, new_hgrn_prompt, new_conv_prompt = _trunk(
        x_prompt.reshape(bsz * seq, d), prompt_h0, prompt_c0, p, ffn_bf16, n_seq=bsz, t=seq)

    y_sample, new_hgrn_sample, new_conv_sample = _trunk(
        x_sample.reshape(dec_b * dec_t, d), state_hgrn, state_conv, p, ffn_bf16, n_seq=dec_b, t=dec_t)

    return (y_prompt.reshape(bsz, seq, d), y_sample.reshape(dec_b, dec_t, d),
            new_hgrn_prompt, new_conv_prompt, new_hgrn_sample, new_conv_sample)
```

```python
import functools

import jax
import jax.numpy as jnp
import numpy as np
from jax import lax
from jax.experimental import pallas as pl
from jax.experimental.pallas import tpu as pltpu

F32, BF16 = jnp.float32, jnp.bfloat16
EPS = 1e-6
CONV_W = 3

V7X_VMEM_BYTES = 64 * 1024 * 1024
V7X_VMEM_COMPILER_RESERVE = 6 * 1024 * 1024
BF16_SUBLANES = 16
LANES = 128
V7X_MXU_WIDTH = 256
SCAN_LEVELS = (2, 4)

ROW_TILE = 1024
MIN_PIPELINED_ROWS = 256
GLA_CHUNK = 64
GLA_LONG_SEQS_PER_STEP = 8
GLA_SEQS_PER_STEP = 16
GLA_SHORT_SEQS_INTERLEAVED = 4
CONV_ROWS = 1024
CONV_COLS = 256
FF_CHUNKS = 4


def _vmem_limit(est_bytes):
    return int(min(max(est_bytes, 16 * 1024 * 1024), V7X_VMEM_BYTES - V7X_VMEM_COMPILER_RESERVE))


def _rmsnorm(x, g):
    return x * lax.rsqrt(jnp.mean(x * x, axis=-1, keepdims=True) + EPS) * g


def _silu(x):
    half = 0.5 * x
    return half + half * jnp.tanh(half)


def _dot(a, b):
    return jnp.dot(a, b, preferred_element_type=F32)


def _dot_nt(a, b):
    return lax.dot_general(a, b, (((1,), (1,)), ((), ())), preferred_element_type=F32)


def _dot_tn(a, b):
    return lax.dot_general(a, b, (((0,), (0,)), ((), ())), preferred_element_type=F32)


def _resident(shape):
    zeros = (0,) * len(shape)
    return pl.BlockSpec(shape, lambda *_: zeros, pipeline_mode=pl.Buffered(1))


def _layer_of(stacked, l):
    index = (l,) + (0,) * (stacked.ndim - 1)
    return pl.BlockSpec((None,) + stacked.shape[1:], lambda *_: index, pipeline_mode=pl.Buffered(1))


def _norm_matmul_body(x_ref, g_ref, w_ref, o_ref, *, n_chunk):
    h = _rmsnorm(x_ref[...], g_ref[...]).astype(BF16)
    for c in range(0, o_ref.shape[1], n_chunk):
        o_ref[:, c:c + n_chunk] = _dot(h, w_ref[:, c:c + n_chunk])


def _norm_matmul(x, g, w, *, l, j, tm):
    r, d = x.shape
    n = w.shape[2]
    est = 2 * tm * d * 4 + d * n * 2 + 2 * tm * n * 4 + tm * d * 8 + tm * 1024 * 8
    return pl.pallas_call(
        functools.partial(_norm_matmul_body, n_chunk=1024),
        grid=(r // tm,),
        in_specs=[pl.BlockSpec((tm, d), lambda i: (i, 0)), _layer_of(g, l), _layer_of(w, j)],
        out_specs=pl.BlockSpec((tm, n), lambda i: (i, 0)),
        out_shape=jax.ShapeDtypeStruct((r, n), F32),
        compiler_params=pltpu.CompilerParams(
            dimension_semantics=("parallel",), vmem_limit_bytes=_vmem_limit(est)),
        name="hgrn_in_proj",
    )(x, g, w)


def _gla_constants(bt, stack):
    i = np.arange(bt)[:, None]
    t = np.arange(bt)[None, :]
    blocks = [t <= i]
    for s in SCAN_LEVELS:
        mid = (i // (2 * s)) * 2 * s + s - 1
        blocks.append(np.where(i <= mid, (t > i) & (t <= mid), (t > mid) & (t <= i)))
    scan = np.concatenate(blocks, axis=0).astype(np.float32)
    r = np.arange(stack)[:, None]
    c = np.arange(LANES)[None, :] + (r // LANES) * LANES
    x = (r % bt) ^ (c % bt)
    lvl = np.where(x == 0, 0, 1 + np.floor(np.log2(np.maximum(x, 1))).astype(np.int32))
    lvl = np.where((r // bt == c // bt) & (r >= c), lvl, -1).astype(np.int32)
    return jnp.asarray(np.concatenate([scan, scan], axis=1), BF16), jnp.asarray(lvl)


def _gla_body(proj_ref, s0_ref, lbl_ref, gn_ref, scan_ref, lvl_ref, *rest,
              layer, nb, t_rows, bt, n_heads, hk, hv, unrolled):
    og_ref, snew_ref, aux_ref, o_ref, w_ref = rest[-5:]
    st_ref = aux_ref if unrolled else None
    hl_ref = None if unrolled else aux_ref
    n_slots = w_ref.shape[0]
    c = pl.program_id(1)
    key = n_heads * hk
    d = n_heads * hv
    pad = bt - t_rows
    stack = lvl_ref.shape[0]
    group = stack // bt
    width = group * hk
    tiles = stack // LANES

    def proj_rows(i, lo, hi):
        if unrolled:
            return proj_ref[i, :, lo:hi]
        return proj_ref[pl.ds(pl.multiple_of(i * t_rows, t_rows), t_rows), lo:hi]

    if unrolled:
        @pl.when(c == 0)
        def _():
            for i in range(nb):
                for h in range(n_heads):
                    st_ref[i, h] = s0_ref[i, h].T

    logits = lbl_ref[...]
    ex = jnp.exp(logits - jnp.max(logits, axis=0, keepdims=True))
    sm = ex / jnp.sum(ex, axis=0, keepdims=True)
    lb = jnp.sum(sm[0:layer + 1], axis=0, keepdims=True) - sm[0:1]

    om = 1.0 - lb

    sub = lax.broadcasted_iota(jnp.int32, (1, 8, width), 1)
    lvl = lvl_ref[...]

    def padded(x):
        return x if pad == 0 else jnp.concatenate([x, jnp.zeros((pad, x.shape[1]), x.dtype)], axis=0)

    def by_role(s, second_half, first_half):
        shape = second_half.shape
        pick = jnp.where((sub & s) != 0, second_half.reshape(bt // 8, 8, width),
                         first_half.reshape(bt // 8, 8, width))
        return pick.reshape(shape)

    def stacked(x):
        return jnp.concatenate([x[:, e * hk:(e + 1) * hk] for e in range(group)], axis=0).astype(BF16)

    def gram(x, y):
        full = _dot_nt(x, y)
        return jnp.concatenate(
            [full[n * LANES:(n + 1) * LANES, n * LANES:(n + 1) * LANES] for n in range(tiles)], axis=0)

    n_stacks = n_heads // group

    def prepare(i, slot):
        wv = w_ref.at[slot]
        q = padded(proj_rows(i, 0, key))
        z = padded(proj_rows(i, key, 2 * key))
        w = om * jax.nn.sigmoid(z)
        f = lb + w
        lf = jnp.log(f)
        k = om - w
        if pad:
            real = lax.broadcasted_iota(jnp.int32, (bt, key), 0) < t_rows
            f = jnp.where(real, f, 1.0)
            lf = jnp.where(real, lf, 0.0)
            k = jnp.where(real, k, 0.0)
        hi = lf.astype(BF16)
        hl = jnp.concatenate([hi, (lf - hi.astype(F32)).astype(BF16)], axis=0)
        if not unrolled:
            hl_ref[slot] = hl
        wv[0] = _silu(q)
        wv[1] = k
        wv[2] = f
        wv[3:3 + 1 + len(SCAN_LEVELS)] = _dot(scan_ref[...], hl).reshape(1 + len(SCAN_LEVELS), bt, key)

    def intra(i, slot, p):
        wv = w_ref.at[slot]
        lanes = slice(p * width, (p + 1) * width)
        qs, k, f, b = wv[0, :, lanes], wv[1, :, lanes], wv[2, :, lanes], wv[3, :, lanes]
        v = stacked(padded(proj_rows(i, 2 * key + p * width, 2 * key + (p + 1) * width)))

        operands = [by_role(1, qs * f, k)]
        for n, s in enumerate(SCAN_LEVELS):
            operands.append(by_role(s, qs, k) * jnp.exp(wv[4 + n, :, lanes]))
        s = 2 * SCAN_LEVELS[-1]
        while s < bt:
            pieces = []
            for m in range(0, bt, 2 * s):
                mid = b[m + s - 1:m + s]
                pieces.append(k[m:m + s] * jnp.exp(mid - b[m:m + s]))
                pieces.append(qs[m + s:m + 2 * s] * jnp.exp(b[m + s:m + 2 * s] - mid))
            operands.append(jnp.concatenate(pieces, axis=0))
            s *= 2

        a = jnp.where(lvl == 0, gram(stacked(qs), stacked(k)), 0.0)
        for n, x in enumerate(operands):
            x = stacked(x)
            a = jnp.where(lvl == n + 1, gram(x, x), a)
        bl = b[bt - 1:bt]
        return (a.astype(BF16), v, (qs * jnp.exp(b)).astype(BF16), (k * jnp.exp(bl - b)).astype(BF16),
                jnp.exp(bl))

    def finish(i, p, a, v, qb, kt, decay):
        rows = pl.ds(i * t_rows if unrolled else pl.multiple_of(i * t_rows, t_rows), t_rows)
        o_intra = jnp.concatenate(
            [_dot(a[n * LANES:(n + 1) * LANES], v[n * LANES:(n + 1) * LANES]) for n in range(tiles)],
            axis=0)
        for e in range(group):
            h = p * group + e
            head = slice(e * hk, (e + 1) * hk)
            st = st_ref[i, h]
            o = o_intra[e * bt:(e + 1) * bt] + _dot_nt(qb[:, head], st.astype(BF16))
            o_ref[rows, h * hv:(h + 1) * hv] = o[0:t_rows]
            st_ref[i, h] = st * decay[:, head] + _dot_tn(v[e * bt:(e + 1) * bt], kt[:, head])

    def finish_in_place(i, slot, a, v, qb, kt):
        rows = pl.ds(pl.multiple_of(i * t_rows, t_rows), t_rows)
        o_intra = _dot(a, v)
        sum_rows = jnp.concatenate([jnp.ones((2 * bt, hv), BF16), jnp.zeros((2 * bt, hv), BF16)], axis=1)
        for h in range(n_heads):
            head = slice(h * hk, (h + 1) * hk)
            s = s0_ref[i, h]
            o = o_intra[h * bt:(h + 1) * bt] + _dot(qb[:, head], s.astype(BF16))
            o_ref[rows, h * hv:(h + 1) * hv] = o[0:t_rows]
            lhs = jnp.concatenate([hl_ref[slot, :, head], kt[:, head]], axis=0)
            rhs = jnp.concatenate(
                [sum_rows,
                 jnp.concatenate([jnp.zeros((bt, hv), BF16), v[h * bt:(h + 1) * bt]], axis=1)], axis=0)
            upd = _dot_tn(lhs, rhs)
            snew_ref[i, h] = jnp.exp(upd[:, 0:hv]) * s + upd[:, hv:2 * hv]

    def interleaved_seqs(j):
        seqs = [j * n_slots + slot for slot in range(n_slots)]
        for slot, i in enumerate(seqs):
            prepare(i, slot)
        parts = [intra(i, slot, 0)[:4] for slot, i in enumerate(seqs)]
        for slot, i in enumerate(seqs):
            finish_in_place(i, slot, *parts[slot])

    if unrolled:
        prepare(0, 0)
        pending = None
        for i in range(nb):
            for p in range(n_stacks):
                if p == 0 and i + 1 < nb:
                    prepare(i + 1, i + 1)
                current = (i, p) + intra(i, i, p)
                if pending is not None:
                    finish(*pending)
                pending = current
        finish(*pending)
    else:
        lax.fori_loop(0, nb // n_slots, lambda j, _: (interleaved_seqs(j), 0)[1], 0)

    gate_lanes = slice(2 * key + d, 2 * key + 2 * d)
    gate = proj_ref[:, :, gate_lanes].reshape(nb * t_rows, d) if unrolled else proj_ref[:, gate_lanes]
    og = (_rmsnorm(o_ref[...], gn_ref[...]) * _silu(gate)).astype(BF16)
    og_ref[...] = og.reshape(og_ref.shape)

    if unrolled:
        @pl.when(c == pl.num_programs(1) - 1)
        def _():
            for i in range(nb):
                for h in range(n_heads):
                    snew_ref[i, h] = st_ref[i, h].T


def _gla(proj, s0, lb_logits, gn, snew, *, layer, n_seq, t):
    _, _, n_heads, hk, hv = s0.shape
    d = n_heads * hv
    key = n_heads * hk
    unrolled = t >= GLA_CHUNK
    if unrolled:
        nb, t_rows = min(GLA_LONG_SEQS_PER_STEP, n_seq), GLA_CHUNK
    else:
        nb, t_rows = min(GLA_SEQS_PER_STEP, n_seq), t
    bt = max(t_rows, BF16_SUBLANES)
    nc = t // t_rows
    rows = nb * t_rows
    assert t % t_rows == 0 and n_seq % nb == 0 and (unrolled or nc == 1) and rows % BF16_SUBLANES == 0
    stack = min(LANES, n_heads * bt)
    assert LANES % bt == 0 and stack % LANES == 0 and (n_heads * bt) % stack == 0 and hk == LANES
    assert bt > 2 * SCAN_LEVELS[-1] and (unrolled or stack == n_heads * bt == LANES)
    scan, lvl = _gla_constants(bt, stack)
    n_planes = 4 + len(SCAN_LEVELS)
    n_slots = nb if unrolled else (GLA_SHORT_SEQS_INTERLEAVED if nb % GLA_SHORT_SEQS_INTERLEAVED == 0 else 1)
    state_bytes = nb * n_heads * hk * hv * 4
    est = (2 * rows * proj.shape[1] * 4 + 5 * state_bytes + 4 * rows * d * 4
           + 3 * n_slots * n_planes * bt * key * 4 + 8 * 1024 * 1024)
    body = functools.partial(_gla_body, layer=layer, nb=nb, t_rows=t_rows, bt=bt,
                             n_heads=n_heads, hk=hk, hv=hv, unrolled=unrolled)
    if unrolled:
        proj = proj.reshape(n_seq, t, proj.shape[1])
        row_block = lambda width: pl.BlockSpec((nb, t_rows, width), lambda b, c: (b, c, 0))
        og_shape = (n_seq, t, d)
    else:
        row_block = lambda width: pl.BlockSpec((rows, width), lambda b, c: (b, 0))
        og_shape = (n_seq * t, d)
    state_spec = pl.BlockSpec((None, nb, n_heads, hk, hv), lambda b, c: (layer, b, 0, 0, 0))
    aliased = () if snew is None else (snew,)
    og, snew = pl.pallas_call(
        body,
        grid=(n_seq // nb, nc),
        in_specs=[row_block(proj.shape[-1]),
                  state_spec, _resident(lb_logits.shape), _layer_of(gn, layer),
                  _resident(scan.shape), _resident(lvl.shape)]
                 + [pl.BlockSpec(memory_space=pl.ANY)] * len(aliased),
        out_specs=[row_block(d), state_spec],
        out_shape=[jax.ShapeDtypeStruct(og_shape, BF16), jax.ShapeDtypeStruct(s0.shape, F32)],
        input_output_aliases={6: 1} if aliased else {},
        scratch_shapes=[pltpu.VMEM((nb, n_heads, hv, hk), F32) if unrolled
                        else pltpu.VMEM((n_slots, 2 * bt, key), BF16),
                        pltpu.VMEM((rows, d), F32),
                        pltpu.VMEM((n_slots, n_planes, bt, key), F32)],
        compiler_params=pltpu.CompilerParams(
            dimension_semantics=("parallel", "arbitrary"), vmem_limit_bytes=_vmem_limit(est)),
        name="hgrn_gla",
    )(proj, s0, lb_logits, gn, scan, lvl, *aliased)
    return og.reshape(n_seq * t, d), snew


def _conv_body(x_ref, g_ref, w_ref, wc_ref, buf_ref, a_ref, nbuf_ref, *, nb, tt):
    c = pl.program_id(1)
    d = x_ref.shape[1]
    rows = nb * tt

    @pl.when(c == 0)
    def _():
        nbuf_ref[...] = buf_ref[...]

    h = _rmsnorm(x_ref[...], g_ref[...]).astype(BF16)
    t = lax.broadcasted_iota(jnp.int32, (nb, tt, CONV_COLS), 1)

    def project(n):
        cols = [slice(k * d + n * CONV_COLS, k * d + (n + 1) * CONV_COLS) for k in range(3)]
        return tuple(_dot(h, w_ref[:, s]) for s in cols)

    def tail(n, gate_b, gate_c, third):
        cols = slice(n * CONV_COLS, (n + 1) * CONV_COLS)
        u = (gate_c * third).reshape(nb, tt, CONV_COLS)
        p2, p1 = nbuf_ref[:, 0:1, cols], nbuf_ref[:, 1:2, cols]
        u1 = jnp.where(t == 0, p1, pltpu.roll(u, 1, axis=1))
        u2 = jnp.where(t == 0, p2, jnp.where(t == 1, p1, pltpu.roll(u, 2, axis=1)))
        w0, w1, w2 = (wc_ref[k:k + 1, cols].reshape(1, 1, CONV_COLS) for k in range(3))
        y = w0 * u2 + w1 * u1 + w2 * u
        a_ref[:, cols] = (gate_b * y.reshape(rows, CONV_COLS)).astype(BF16)
        nbuf_ref[:, :, cols] = u[:, tt - 2:tt, :]

    pending = None
    for n in range(d // CONV_COLS):
        current = (n,) + project(n)
        if pending is not None:
            tail(*pending)
        pending = current
    tail(*pending)


def _conv_mixer(x, g, w_in, w_conv, buf, *, l, j, n_seq, t):
    r, d = x.shape
    if t >= CONV_ROWS:
        nb, tt = 1, CONV_ROWS
    else:
        nb, tt = min(CONV_ROWS // t, n_seq), t
        if nb == n_seq and nb * t >= 2 * MIN_PIPELINED_ROWS and nb % 2 == 0:
            nb //= 2
    nc = t // tt
    rows = nb * tt
    assert t % tt == 0 and n_seq % nb == 0 and (nb == 1 or nc == 1) and tt % 8 == 0 and tt >= CONV_W - 1
    est = 2 * rows * d * 4 + d * 3 * d * 2 + 2 * rows * d * 2 + 10 * rows * d * 4
    buf_spec = pl.BlockSpec((None, nb, CONV_W - 1, d), lambda b, c: (j, b, 0, 0))
    nbuf_spec = pl.BlockSpec((nb, CONV_W - 1, d), lambda b, c: (b, 0, 0))
    return pl.pallas_call(
        functools.partial(_conv_body, nb=nb, tt=tt),
        grid=(n_seq // nb, nc),
        in_specs=[pl.BlockSpec((rows, d), lambda b, c: (b * nc + c, 0)), _layer_of(g, l),
                  _layer_of(w_in, j), _layer_of(w_conv, j), buf_spec],
        out_specs=[pl.BlockSpec((rows, d), lambda b, c: (b * nc + c, 0)), nbuf_spec],
        out_shape=[jax.ShapeDtypeStruct((r, d), BF16), jax.ShapeDtypeStruct(buf.shape[1:], F32)],
        compiler_params=pltpu.CompilerParams(
            dimension_semantics=("parallel", "arbitrary"), vmem_limit_bytes=_vmem_limit(est)),
        name="conv_mixer",
    )(x, g, w_in, w_conv, buf)


def _ff_chunks(dff):
    tiles = dff // V7X_MXU_WIDTH
    assert tiles * V7X_MXU_WIDTH == dff
    bounds = [round(n * tiles / FF_CHUNKS) * V7X_MXU_WIDTH for n in range(FF_CHUNKS + 1)]
    return tuple(zip(bounds[:-1], bounds[1:]))


def _out_ffn_body(x_ref, a_ref, wo_ref, g_ref, wg_ref, wu_ref, wd_ref, gf_ref, y_ref, *,
                  ff_chunks, final):
    x = x_ref[...] + _dot(a_ref[...], wo_ref[...])
    h = _rmsnorm(x, g_ref[...]).astype(BF16)
    for lo, hi in ff_chunks:
        act = _silu(_dot(h, wg_ref[:, lo:hi])) * _dot(h, wu_ref[:, lo:hi])
        x = x + _dot(act.astype(BF16), wd_ref[lo:hi, :])
    y_ref[...] = _rmsnorm(x, gf_ref[...]) if final else x


def _out_ffn(x, a, w_out, g, w_gate, w_up, w_down, g_final, *, l, j, tm, final):
    r, d = x.shape
    dff = w_gate.shape[1]
    ff_chunks = _ff_chunks(dff)
    widest = max(hi - lo for lo, hi in ff_chunks)
    weights = (d * d + 3 * d * dff) * 2
    est = 2 * tm * d * (4 + 2 + 4) + weights + 3 * tm * d * 4 + tm * widest * 14
    row = lambda i: (i, 0)
    return pl.pallas_call(
        functools.partial(_out_ffn_body, ff_chunks=ff_chunks, final=final),
        grid=(r // tm,),
        in_specs=[pl.BlockSpec((tm, d), row), pl.BlockSpec((tm, d), row), _layer_of(w_out, j),
                  _layer_of(g, l), _resident(w_gate.shape), _resident(w_up.shape), _resident(w_down.shape),
                  _resident(g_final.shape)],
        out_specs=pl.BlockSpec((tm, d), row),
        out_shape=jax.ShapeDtypeStruct((r, d), F32),
        compiler_params=pltpu.CompilerParams(
            dimension_semantics=("parallel",), vmem_limit_bytes=_vmem_limit(est)),
        name="out_ffn",
    )(x, a, w_out, g, w_gate, w_up, w_down, g_final)


def _out_ffn_cast_body(x_ref, a_ref, wo_ref, g_ref, wg_ref, wu_ref, wd_ref, gf_ref,
                       y_ref, wgb_ref, wub_ref, wdb_ref, h_ref, acc_ref, *, final):
    c = pl.program_id(0)

    @pl.when(c == 0)
    def _():
        x = x_ref[...] + _dot(a_ref[...], wo_ref[...])
        h_ref[...] = _rmsnorm(x, g_ref[...]).astype(BF16)
        acc_ref[...] = x

    wg, wu, wd = wg_ref[...].astype(BF16), wu_ref[...].astype(BF16), wd_ref[...].astype(BF16)
    wgb_ref[...], wub_ref[...], wdb_ref[...] = wg, wu, wd
    h = h_ref[...]
    acc_ref[...] += _dot((_silu(_dot(h, wg)) * _dot(h, wu)).astype(BF16), wd)

    @pl.when(c == pl.num_programs(0) - 1)
    def _():
        y_ref[...] = _rmsnorm(acc_ref[...], gf_ref[...]) if final else acc_ref[...]


def _out_ffn_cast(x, a, w_out, g, w_gate, w_up, w_down, g_final, *, l, j, final):
    r, d = x.shape
    dff = w_gate.shape[2]
    fc = V7X_MXU_WIDTH
    assert dff % fc == 0
    est = 4 * r * d * 4 + d * d * 2 + 2 * 3 * d * fc * (4 + 2) + 3 * d * fc * 2 + 8 * r * fc * 4
    whole = lambda shape: pl.BlockSpec(shape, lambda c: (0,) * len(shape))
    y, wg, wu, wd = pl.pallas_call(
        functools.partial(_out_ffn_cast_body, final=final),
        grid=(dff // fc,),
        in_specs=[whole((r, d)), whole((r, d)), _layer_of(w_out, j), _layer_of(g, l),
                  pl.BlockSpec((None, d, fc), lambda c: (l, 0, c)),
                  pl.BlockSpec((None, d, fc), lambda c: (l, 0, c)),
                  pl.BlockSpec((None, fc, d), lambda c: (l, c, 0)),
                  _resident(g_final.shape)],
        out_specs=[whole((r, d)), pl.BlockSpec((d, fc), lambda c: (0, c)),
                   pl.BlockSpec((d, fc), lambda c: (0, c)), pl.BlockSpec((fc, d), lambda c: (c, 0))],
        out_shape=[jax.ShapeDtypeStruct((r, d), F32), jax.ShapeDtypeStruct((d, dff), BF16),
                   jax.ShapeDtypeStruct((d, dff), BF16), jax.ShapeDtypeStruct((dff, d), BF16)],
        scratch_shapes=[pltpu.VMEM((r, d), BF16), pltpu.VMEM((r, d), F32)],
        compiler_params=pltpu.CompilerParams(
            dimension_semantics=("arbitrary",), vmem_limit_bytes=_vmem_limit(est)),
        name="out_ffn_cast",
    )(x, a, w_out, g, w_gate, w_up, w_down, g_final)
    return y, (wg, wu, wd)


def _trunk(x, s_hgrn, s_conv, p, ffn_bf16, *, n_seq, t):
    depth = p["norm_mix"].shape[0]
    rows = x.shape[0]
    tm = min(ROW_TILE, rows // 2) if rows >= 2 * MIN_PIPELINED_ROWS else rows
    assert rows % tm == 0
    new_h, new_c = None, []
    for l in range(depth):
        j = l // 2
        if l % 2 == 0:
            proj = _norm_matmul(x, p["norm_mix"], p["hgrn_w_in"], l=l, j=j, tm=tm)
            a, new_h = _gla(proj, s_hgrn, p["hgrn_lb_logits"], p["hgrn_norm"], new_h,
                            layer=j, n_seq=n_seq, t=t)
            w_out = p["hgrn_w_out"]
        else:
            a, s = _conv_mixer(x, p["norm_mix"], p["conv_w_in"], p["conv_w"], s_conv,
                               l=l, j=j, n_seq=n_seq, t=t)
            new_c.append(s)
            w_out = p["conv_w_out"]
        if len(ffn_bf16) <= l:
            x, cast = _out_ffn_cast(x, a, w_out, p["norm_ffn"], p["ffn_w_gate"], p["ffn_w_up"],
                                    p["ffn_w_down"], p["norm_final"], l=l, j=j, final=(l == depth - 1))
            ffn_bf16.append(cast)
        else:
            x = _out_ffn(x, a, w_out, p["norm_ffn"], *ffn_bf16[l], p["norm_final"],
                         l=l, j=j, tm=tm, final=(l == depth - 1))
    return x, new_h, jnp.stack(new_c)


def kernel(x_prompt, x_sample, state_hgrn, state_conv, meta_tokens, norm_mix, norm_ffn, norm_final,
           hgrn_w_in, hgrn_w_out, hgrn_lb_logits, hgrn_norm, conv_w_in, conv_w, conv_w_out,
           ffn_w_gate, ffn_w_up, ffn_w_down):
    bsz, seq, d = x_prompt.shape
    dec_b, dec_t, _ = x_sample.shape
    n_meta = meta_tokens.shape[0]
    p = dict(norm_mix=norm_mix[:, None], norm_ffn=norm_ffn[:, None], norm_final=norm_final[None],
             hgrn_lb_logits=hgrn_lb_logits, hgrn_norm=hgrn_norm[:, None], conv_w=conv_w,
             hgrn_w_in=hgrn_w_in.astype(BF16), hgrn_w_out=hgrn_w_out.astype(BF16),
             conv_w_in=conv_w_in.astype(BF16), conv_w_out=conv_w_out.astype(BF16),
             ffn_w_gate=ffn_w_gate, ffn_w_up=ffn_w_up, ffn_w_down=ffn_w_down)
    ffn_bf16 = []

    zeros_h = jnp.zeros((state_hgrn.shape[0], 1) + state_hgrn.shape[2:], F32)
    zeros_c = jnp.zeros((state_conv.shape[0], 1) + state_conv.shape[2:], F32)
    _, meta_h, meta_c = _trunk(meta_tokens.astype(F32), zeros_h, zeros_c, p, ffn_bf16, n_seq=1, t=n_meta)

    prompt_h0 = jnp.broadcast_to(meta_h, (meta_h.shape[0], bsz) + meta_h.shape[2:])
    prompt_c0 = jnp.broadcast_to(meta_c, (meta_c.shape[0], bsz) + meta_c.shape[2:])
    y_prompt, new_hgrn_prompt, new_conv_prompt = _trunk(
        x_prompt.reshape(bsz * seq, d), prompt_h0, prompt_c0, p, ffn_bf16, n_seq=bsz, t=seq)

    y_sample, new_hgrn_sample, new_conv_sample = _trunk(
        x_sample.reshape(dec_b * dec_t, d), state_hgrn, state_conv, p, ffn_bf16, n_seq=dec_b, t=dec_t)

    return (y_prompt.reshape(bsz, seq, d), y_sample.reshape(dec_b, dec_t, d),
            new_hgrn_prompt, new_conv_prompt, new_hgrn_sample, new_conv_sample)
```

```python
import functools

import jax
import jax.numpy as jnp
import numpy as np
from jax import lax
from jax.experimental import pallas as pl
from jax.experimental.pallas import tpu as pltpu

F32, BF16 = jnp.float32, jnp.bfloat16
EPS = 1e-6
CONV_W = 3

V7X_VMEM_BYTES = 64 * 1024 * 1024
V7X_VMEM_COMPILER_RESERVE = 6 * 1024 * 1024
BF16_SUBLANES = 16
LANES = 128
V7X_MXU_WIDTH = 256
SCAN_LEVELS = (2, 4)

ROW_TILE = 1024
MIN_PIPELINED_ROWS = 256
GLA_CHUNK = 64
GLA_LONG_SEQS_PER_STEP = 8
GLA_SEQS_PER_STEP = 8
STATE_RING = 3
GLA_SHORT_SEQS_INTERLEAVED = 4
CONV_ROWS = 1024
CONV_COLS = 256
FF_CHUNKS = 4


def _vmem_limit(est_bytes):
    return int(min(max(est_bytes, 16 * 1024 * 1024), V7X_VMEM_BYTES - V7X_VMEM_COMPILER_RESERVE))


def _rmsnorm(x, g):
    return x * lax.rsqrt(jnp.mean(x * x, axis=-1, keepdims=True) + EPS) * g


def _silu(x):
    half = 0.5 * x
    return half + half * jnp.tanh(half)


def _dot(a, b):
    return jnp.dot(a, b, preferred_element_type=F32)


def _dot_nt(a, b):
    return lax.dot_general(a, b, (((1,), (1,)), ((), ())), preferred_element_type=F32)


def _dot_tn(a, b):
    return lax.dot_general(a, b, (((0,), (0,)), ((), ())), preferred_element_type=F32)


def _resident(shape):
    zeros = (0,) * len(shape)
    return pl.BlockSpec(shape, lambda *_: zeros, pipeline_mode=pl.Buffered(1))


def _layer_of(stacked, l):
    index = (l,) + (0,) * (stacked.ndim - 1)
    return pl.BlockSpec((None,) + stacked.shape[1:], lambda *_: index, pipeline_mode=pl.Buffered(1))


def _norm_matmul_body(x_ref, g_ref, w_ref, o_ref, *, n_chunk):
    h = _rmsnorm(x_ref[...], g_ref[...]).astype(BF16)
    for c in range(0, o_ref.shape[1], n_chunk):
        o_ref[:, c:c + n_chunk] = _dot(h, w_ref[:, c:c + n_chunk])


def _norm_matmul(x, g, w, *, l, j, tm):
    r, d = x.shape
    n = w.shape[2]
    est = 2 * tm * d * 4 + d * n * 2 + 2 * tm * n * 4 + tm * d * 8 + tm * 1024 * 8
    return pl.pallas_call(
        functools.partial(_norm_matmul_body, n_chunk=1024),
        grid=(r // tm,),
        in_specs=[pl.BlockSpec((tm, d), lambda i: (i, 0)), _layer_of(g, l), _layer_of(w, j)],
        out_specs=pl.BlockSpec((tm, n), lambda i: (i, 0)),
        out_shape=jax.ShapeDtypeStruct((r, n), F32),
        compiler_params=pltpu.CompilerParams(
            dimension_semantics=("parallel",), vmem_limit_bytes=_vmem_limit(est)),
        name="hgrn_in_proj",
    )(x, g, w)


def _gla_constants(bt, stack):
    i = np.arange(bt)[:, None]
    t = np.arange(bt)[None, :]
    blocks = [t <= i]
    for s in SCAN_LEVELS:
        mid = (i // (2 * s)) * 2 * s + s - 1
        blocks.append(np.where(i <= mid, (t > i) & (t <= mid), (t > mid) & (t <= i)))
    scan = np.concatenate(blocks, axis=0).astype(np.float32)
    r = np.arange(stack)[:, None]
    c = np.arange(LANES)[None, :] + (r // LANES) * LANES
    x = (r % bt) ^ (c % bt)
    lvl = np.where(x == 0, 0, 1 + np.floor(np.log2(np.maximum(x, 1))).astype(np.int32))
    lvl = np.where((r // bt == c // bt) & (r >= c), lvl, -1).astype(np.int32)
    return jnp.asarray(np.concatenate([scan, scan], axis=1), BF16), jnp.asarray(lvl)


def _gla_body(proj_ref, s0_ref, lbl_ref, gn_ref, scan_ref, lvl_ref, *rest,
              layer, nb, t_rows, bt, n_heads, hk, hv, unrolled):
    if unrolled:
        og_ref, snew_ref, st_ref, o_ref, w_ref = rest[-5:]
        hl_ref = ring_ref = ring_sem = None
    else:
        og_ref, snew_ref, hl_ref, o_ref, w_ref, ring_ref, ring_sem = rest[-7:]
        st_ref = None
    n_slots = w_ref.shape[0]
    c = pl.program_id(1)
    key = n_heads * hk
    d = n_heads * hv
    pad = bt - t_rows
    stack = lvl_ref.shape[0]
    group = stack // bt
    width = group * hk
    tiles = stack // LANES

    if not unrolled:
        step, n_steps = pl.program_id(0), pl.num_programs(0)

        def state_copy(s):
            slot = lax.rem(s, STATE_RING)
            return pltpu.make_async_copy(s0_ref.at[layer, pl.ds(s * nb, nb)], ring_ref.at[slot],
                                         ring_sem.at[slot])

        @pl.when(step == 0)
        def _():
            for s in range(STATE_RING - 1):
                @pl.when(s < n_steps)
                def _():
                    state_copy(s).start()

        @pl.when(step + STATE_RING - 1 < n_steps)
        def _():
            state_copy(step + STATE_RING - 1).start()

        state_copy(step).wait()
        ring_slot = lax.rem(step, STATE_RING)

    def proj_rows(i, lo, hi):
        if unrolled:
            return proj_ref[i, :, lo:hi]
        return proj_ref[pl.ds(pl.multiple_of(i * t_rows, t_rows), t_rows), lo:hi]

    if unrolled:
        @pl.when(c == 0)
        def _():
            for i in range(nb):
                for h in range(n_heads):
                    st_ref[i, h] = s0_ref[i, h].T

    logits = lbl_ref[...]
    ex = jnp.exp(logits - jnp.max(logits, axis=0, keepdims=True))
    sm = ex / jnp.sum(ex, axis=0, keepdims=True)
    lb = jnp.sum(sm[0:layer + 1], axis=0, keepdims=True) - sm[0:1]

    om = 1.0 - lb

    sub = lax.broadcasted_iota(jnp.int32, (1, 8, width), 1)
    lvl = lvl_ref[...]

    def padded(x):
        return x if pad == 0 else jnp.concatenate([x, jnp.zeros((pad, x.shape[1]), x.dtype)], axis=0)

    def by_role(s, second_half, first_half):
        shape = second_half.shape
        pick = jnp.where((sub & s) != 0, second_half.reshape(bt // 8, 8, width),
                         first_half.reshape(bt // 8, 8, width))
        return pick.reshape(shape)

    def stacked(x):
        return jnp.concatenate([x[:, e * hk:(e + 1) * hk] for e in range(group)], axis=0).astype(BF16)

    def gram(x, y):
        full = _dot_nt(x, y)
        return jnp.concatenate(
            [full[n * LANES:(n + 1) * LANES, n * LANES:(n + 1) * LANES] for n in range(tiles)], axis=0)

    n_stacks = n_heads // group

    def prepare(i, slot):
        wv = w_ref.at[slot]
        q = padded(proj_rows(i, 0, key))
        z = padded(proj_rows(i, key, 2 * key))
        w = om * jax.nn.sigmoid(z)
        f = lb + w
        lf = jnp.log(f)
        k = om - w
        if pad:
            real = lax.broadcasted_iota(jnp.int32, (bt, key), 0) < t_rows
            f = jnp.where(real, f, 1.0)
            lf = jnp.where(real, lf, 0.0)
            k = jnp.where(real, k, 0.0)
        hi = lf.astype(BF16)
        hl = jnp.concatenate([hi, (lf - hi.astype(F32)).astype(BF16)], axis=0)
        if not unrolled:
            hl_ref[slot] = hl
        wv[0] = _silu(q)
        wv[1] = k
        wv[2] = f
        wv[3:3 + 1 + len(SCAN_LEVELS)] = _dot(scan_ref[...], hl).reshape(1 + len(SCAN_LEVELS), bt, key)

    def intra(i, slot, p):
        wv = w_ref.at[slot]
        lanes = slice(p * width, (p + 1) * width)
        qs, k, f, b = wv[0, :, lanes], wv[1, :, lanes], wv[2, :, lanes], wv[3, :, lanes]
        v = stacked(padded(proj_rows(i, 2 * key + p * width, 2 * key + (p + 1) * width)))

        operands = [by_role(1, qs * f, k)]
        for n, s in enumerate(SCAN_LEVELS):
            operands.append(by_role(s, qs, k) * jnp.exp(wv[4 + n, :, lanes]))
        s = 2 * SCAN_LEVELS[-1]
        while s < bt:
            pieces = []
            for m in range(0, bt, 2 * s):
                mid = b[m + s - 1:m + s]
                pieces.append(k[m:m + s] * jnp.exp(mid - b[m:m + s]))
                pieces.append(qs[m + s:m + 2 * s] * jnp.exp(b[m + s:m + 2 * s] - mid))
            operands.append(jnp.concatenate(pieces, axis=0))
            s *= 2

        a = jnp.where(lvl == 0, gram(stacked(qs), stacked(k)), 0.0)
        for n, x in enumerate(operands):
            x = stacked(x)
            a = jnp.where(lvl == n + 1, gram(x, x), a)
        bl = b[bt - 1:bt]
        return (a.astype(BF16), v, (qs * jnp.exp(b)).astype(BF16), (k * jnp.exp(bl - b)).astype(BF16),
                jnp.exp(bl))

    def finish(i, p, a, v, qb, kt, decay):
        rows = pl.ds(i * t_rows if unrolled else pl.multiple_of(i * t_rows, t_rows), t_rows)
        o_intra = jnp.concatenate(
            [_dot(a[n * LANES:(n + 1) * LANES], v[n * LANES:(n + 1) * LANES]) for n in range(tiles)],
            axis=0)
        for e in range(group):
            h = p * group + e
            head = slice(e * hk, (e + 1) * hk)
            st = st_ref[i, h]
            o = o_intra[e * bt:(e + 1) * bt] + _dot_nt(qb[:, head], st.astype(BF16))
            o_ref[rows, h * hv:(h + 1) * hv] = o[0:t_rows]
            st_ref[i, h] = st * decay[:, head] + _dot_tn(v[e * bt:(e + 1) * bt], kt[:, head])

    def finish_in_place(i, slot, a, v, qb, kt):
        rows = pl.ds(pl.multiple_of(i * t_rows, t_rows), t_rows)
        o_intra = _dot(a, v)
        sum_rows = jnp.concatenate([jnp.ones((2 * bt, hv), BF16), jnp.zeros((2 * bt, hv), BF16)], axis=1)
        for h in range(n_heads):
            head = slice(h * hk, (h + 1) * hk)
            s = ring_ref[ring_slot, i, h]
            o = o_intra[h * bt:(h + 1) * bt] + _dot(qb[:, head], s.astype(BF16))
            o_ref[rows, h * hv:(h + 1) * hv] = o[0:t_rows]
            lhs = jnp.concatenate([hl_ref[slot, :, head], kt[:, head]], axis=0)
            rhs = jnp.concatenate(
                [sum_rows,
                 jnp.concatenate([jnp.zeros((bt, hv), BF16), v[h * bt:(h + 1) * bt]], axis=1)], axis=0)
            upd = _dot_tn(lhs, rhs)
            snew_ref[i, h] = jnp.exp(upd[:, 0:hv]) * s + upd[:, hv:2 * hv]

    def interleaved_seqs(j):
        seqs = [j * n_slots + slot for slot in range(n_slots)]
        for slot, i in enumerate(seqs):
            prepare(i, slot)
        parts = [intra(i, slot, 0)[:4] for slot, i in enumerate(seqs)]
        for slot, i in enumerate(seqs):
            finish_in_place(i, slot, *parts[slot])

    if unrolled:
        prepare(0, 0)
        pending = None
        for i in range(nb):
            for p in range(n_stacks):
                if p == 0 and i + 1 < nb:
                    prepare(i + 1, i + 1)
                current = (i, p) + intra(i, i, p)
                if pending is not None:
                    finish(*pending)
                pending = current
        finish(*pending)
    else:
        lax.fori_loop(0, nb // n_slots, lambda j, _: (interleaved_seqs(j), 0)[1], 0)

    gate_lanes = slice(2 * key + d, 2 * key + 2 * d)
    gate = proj_ref[:, :, gate_lanes].reshape(nb * t_rows, d) if unrolled else proj_ref[:, gate_lanes]
    og = (_rmsnorm(o_ref[...], gn_ref[...]) * _silu(gate)).astype(BF16)
    og_ref[...] = og.reshape(og_ref.shape)

    if unrolled:
        @pl.when(c == pl.num_programs(1) - 1)
        def _():
            for i in range(nb):
                for h in range(n_heads):
                    snew_ref[i, h] = st_ref[i, h].T


def _gla(proj, s0, lb_logits, gn, snew, *, layer, n_seq, t):
    _, _, n_heads, hk, hv = s0.shape
    d = n_heads * hv
    key = n_heads * hk
    unrolled = t >= GLA_CHUNK
    if unrolled:
        nb, t_rows = min(GLA_LONG_SEQS_PER_STEP, n_seq), GLA_CHUNK
    else:
        nb, t_rows = min(GLA_SEQS_PER_STEP, n_seq), t
    bt = max(t_rows, BF16_SUBLANES)
    nc = t // t_rows
    rows = nb * t_rows
    assert t % t_rows == 0 and n_seq % nb == 0 and (unrolled or nc == 1) and rows % BF16_SUBLANES == 0
    stack = min(V7X_MXU_WIDTH, n_heads * bt)
    assert LANES % bt == 0 and stack % LANES == 0 and (n_heads * bt) % stack == 0 and hk == LANES
    assert bt > 2 * SCAN_LEVELS[-1] and (unrolled or stack == n_heads * bt == LANES)
    scan, lvl = _gla_constants(bt, stack)
    n_planes = 4 + len(SCAN_LEVELS)
    n_slots = nb if unrolled else (GLA_SHORT_SEQS_INTERLEAVED if nb % GLA_SHORT_SEQS_INTERLEAVED == 0 else 1)
    state_bytes = nb * n_heads * hk * hv * 4
    est = (2 * rows * proj.shape[1] * 4 + 5 * state_bytes + 4 * rows * d * 4
           + 3 * n_slots * n_planes * bt * key * 4 + 8 * 1024 * 1024)
    body = functools.partial(_gla_body, layer=layer, nb=nb, t_rows=t_rows, bt=bt,
                             n_heads=n_heads, hk=hk, hv=hv, unrolled=unrolled)
    if unrolled:
        proj = proj.reshape(n_seq, t, proj.shape[1])
        row_block = lambda width: pl.BlockSpec((nb, t_rows, width), lambda b, c: (b, c, 0))
        og_shape = (n_seq, t, d)
    else:
        row_block = lambda width: pl.BlockSpec((rows, width), lambda b, c: (b, 0))
        og_shape = (n_seq * t, d)
    state_spec = pl.BlockSpec((None, nb, n_heads, hk, hv), lambda b, c: (layer, b, 0, 0, 0))
    s0_spec = state_spec if unrolled else pl.BlockSpec(memory_space=pl.ANY)
    aliased = () if snew is None else (snew,)
    og, snew = pl.pallas_call(
        body,
        grid=(n_seq // nb, nc),
        in_specs=[row_block(proj.shape[-1]),
                  s0_spec, _resident(lb_logits.shape), _layer_of(gn, layer),
                  _resident(scan.shape), _resident(lvl.shape)]
                 + [pl.BlockSpec(memory_space=pl.ANY)] * len(aliased),
        out_specs=[row_block(d), state_spec],
        out_shape=[jax.ShapeDtypeStruct(og_shape, BF16), jax.ShapeDtypeStruct(s0.shape, F32)],
        input_output_aliases={6: 1} if aliased else {},
        scratch_shapes=[pltpu.VMEM((nb, n_heads, hv, hk), F32) if unrolled
                        else pltpu.VMEM((n_slots, 2 * bt, key), BF16),
                        pltpu.VMEM((rows, d), F32),
                        pltpu.VMEM((n_slots, n_planes, bt, key), F32)]
                       + ([] if unrolled else [pltpu.VMEM((STATE_RING, nb, n_heads, hk, hv), F32),
                                               pltpu.SemaphoreType.DMA((STATE_RING,))]),
        compiler_params=pltpu.CompilerParams(
            dimension_semantics=("parallel", "arbitrary") if unrolled else ("arbitrary", "arbitrary"),
            vmem_limit_bytes=_vmem_limit(est)),
        name="hgrn_gla",
    )(proj, s0, lb_logits, gn, scan, lvl, *aliased)
    return og.reshape(n_seq * t, d), snew


def _conv_body(x_ref, g_ref, w_ref, wc_ref, buf_ref, a_ref, nbuf_ref, *, nb, tt):
    c = pl.program_id(1)
    d = x_ref.shape[1]
    rows = nb * tt

    @pl.when(c == 0)
    def _():
        nbuf_ref[...] = buf_ref[...]

    h = _rmsnorm(x_ref[...], g_ref[...]).astype(BF16)
    t = lax.broadcasted_iota(jnp.int32, (nb, tt, CONV_COLS), 1)

    def project(n):
        cols = [slice(k * d + n * CONV_COLS, k * d + (n + 1) * CONV_COLS) for k in range(3)]
        return tuple(_dot(h, w_ref[:, s]) for s in cols)

    def tail(n, gate_b, gate_c, third):
        cols = slice(n * CONV_COLS, (n + 1) * CONV_COLS)
        u = (gate_c * third).reshape(nb, tt, CONV_COLS)
        p2, p1 = nbuf_ref[:, 0:1, cols], nbuf_ref[:, 1:2, cols]
        u1 = jnp.where(t == 0, p1, pltpu.roll(u, 1, axis=1))
        u2 = jnp.where(t == 0, p2, jnp.where(t == 1, p1, pltpu.roll(u, 2, axis=1)))
        w0, w1, w2 = (wc_ref[k:k + 1, cols].reshape(1, 1, CONV_COLS) for k in range(3))
        y = w0 * u2 + w1 * u1 + w2 * u
        a_ref[:, cols] = (gate_b * y.reshape(rows, CONV_COLS)).astype(BF16)
        nbuf_ref[:, :, cols] = u[:, tt - 2:tt, :]

    pending = None
    for n in range(d // CONV_COLS):
        current = (n,) + project(n)
        if pending is not None:
            tail(*pending)
        pending = current
    tail(*pending)


def _conv_mixer(x, g, w_in, w_conv, buf, *, l, j, n_seq, t):
    r, d = x.shape
    if t >= CONV_ROWS:
        nb, tt = 1, CONV_ROWS
    else:
        nb, tt = min(CONV_ROWS // t, n_seq), t
        if nb == n_seq and nb * t >= 2 * MIN_PIPELINED_ROWS and nb % 2 == 0:
            nb //= 2
    nc = t // tt
    rows = nb * tt
    assert t % tt == 0 and n_seq % nb == 0 and (nb == 1 or nc == 1) and tt % 8 == 0 and tt >= CONV_W - 1
    est = 2 * rows * d * 4 + d * 3 * d * 2 + 2 * rows * d * 2 + 10 * rows * d * 4
    buf_spec = pl.BlockSpec((None, nb, CONV_W - 1, d), lambda b, c: (j, b, 0, 0))
    nbuf_spec = pl.BlockSpec((nb, CONV_W - 1, d), lambda b, c: (b, 0, 0))
    return pl.pallas_call(
        functools.partial(_conv_body, nb=nb, tt=tt),
        grid=(n_seq // nb, nc),
        in_specs=[pl.BlockSpec((rows, d), lambda b, c: (b * nc + c, 0)), _layer_of(g, l),
                  _layer_of(w_in, j), _layer_of(w_conv, j), buf_spec],
        out_specs=[pl.BlockSpec((rows, d), lambda b, c: (b * nc + c, 0)), nbuf_spec],
        out_shape=[jax.ShapeDtypeStruct((r, d), BF16), jax.ShapeDtypeStruct(buf.shape[1:], F32)],
        compiler_params=pltpu.CompilerParams(
            dimension_semantics=("parallel", "arbitrary"), vmem_limit_bytes=_vmem_limit(est)),
        name="conv_mixer",
    )(x, g, w_in, w_conv, buf)


def _ff_chunks(dff):
    tiles = dff // V7X_MXU_WIDTH
    assert tiles * V7X_MXU_WIDTH == dff
    bounds = [round(n * tiles / FF_CHUNKS) * V7X_MXU_WIDTH for n in range(FF_CHUNKS + 1)]
    return tuple(zip(bounds[:-1], bounds[1:]))


def _out_ffn_body(x_ref, a_ref, wo_ref, g_ref, wg_ref, wu_ref, wd_ref, gf_ref, y_ref, *,
                  ff_chunks, final):
    x = x_ref[...] + _dot(a_ref[...], wo_ref[...])
    h = _rmsnorm(x, g_ref[...]).astype(BF16)
    for lo, hi in ff_chunks:
        act = _silu(_dot(h, wg_ref[:, lo:hi])) * _dot(h, wu_ref[:, lo:hi])
        x = x + _dot(act.astype(BF16), wd_ref[lo:hi, :])
    y_ref[...] = _rmsnorm(x, gf_ref[...]) if final else x


def _out_ffn(x, a, w_out, g, w_gate, w_up, w_down, g_final, *, l, j, tm, final):
    r, d = x.shape
    dff = w_gate.shape[1]
    ff_chunks = _ff_chunks(dff)
    widest = max(hi - lo for lo, hi in ff_chunks)
    weights = (d * d + 3 * d * dff) * 2
    est = 2 * tm * d * (4 + 2 + 4) + weights + 3 * tm * d * 4 + tm * widest * 14
    row = lambda i: (i, 0)
    return pl.pallas_call(
        functools.partial(_out_ffn_body, ff_chunks=ff_chunks, final=final),
        grid=(r // tm,),
        in_specs=[pl.BlockSpec((tm, d), row), pl.BlockSpec((tm, d), row), _layer_of(w_out, j),
                  _layer_of(g, l), _resident(w_gate.shape), _resident(w_up.shape), _resident(w_down.shape),
                  _resident(g_final.shape)],
        out_specs=pl.BlockSpec((tm, d), row),
        out_shape=jax.ShapeDtypeStruct((r, d), F32),
        compiler_params=pltpu.CompilerParams(
            dimension_semantics=("parallel",), vmem_limit_bytes=_vmem_limit(est)),
        name="out_ffn",
    )(x, a, w_out, g, w_gate, w_up, w_down, g_final)


def _out_ffn_cast_body(x_ref, a_ref, wo_ref, g_ref, wg_ref, wu_ref, wd_ref, gf_ref,
                       y_ref, wgb_ref, wub_ref, wdb_ref, h_ref, acc_ref, *, final):
    c = pl.program_id(0)

    @pl.when(c == 0)
    def _():
        x = x_ref[...] + _dot(a_ref[...], wo_ref[...])
        h_ref[...] = _rmsnorm(x, g_ref[...]).astype(BF16)
        acc_ref[...] = x

    wg, wu, wd = wg_ref[...].astype(BF16), wu_ref[...].astype(BF16), wd_ref[...].astype(BF16)
    wgb_ref[...], wub_ref[...], wdb_ref[...] = wg, wu, wd
    h = h_ref[...]
    acc_ref[...] += _dot((_silu(_dot(h, wg)) * _dot(h, wu)).astype(BF16), wd)

    @pl.when(c == pl.num_programs(0) - 1)
    def _():
        y_ref[...] = _rmsnorm(acc_ref[...], gf_ref[...]) if final else acc_ref[...]


def _out_ffn_cast(x, a, w_out, g, w_gate, w_up, w_down, g_final, *, l, j, final):
    r, d = x.shape
    dff = w_gate.shape[2]
    fc = V7X_MXU_WIDTH
    assert dff % fc == 0
    est = 4 * r * d * 4 + d * d * 2 + 2 * 3 * d * fc * (4 + 2) + 3 * d * fc * 2 + 8 * r * fc * 4
    whole = lambda shape: pl.BlockSpec(shape, lambda c: (0,) * len(shape))
    y, wg, wu, wd = pl.pallas_call(
        functools.partial(_out_ffn_cast_body, final=final),
        grid=(dff // fc,),
        in_specs=[whole((r, d)), whole((r, d)), _layer_of(w_out, j), _layer_of(g, l),
                  pl.BlockSpec((None, d, fc), lambda c: (l, 0, c)),
                  pl.BlockSpec((None, d, fc), lambda c: (l, 0, c)),
                  pl.BlockSpec((None, fc, d), lambda c: (l, c, 0)),
                  _resident(g_final.shape)],
        out_specs=[whole((r, d)), pl.BlockSpec((d, fc), lambda c: (0, c)),
                   pl.BlockSpec((d, fc), lambda c: (0, c)), pl.BlockSpec((fc, d), lambda c: (c, 0))],
        out_shape=[jax.ShapeDtypeStruct((r, d), F32), jax.ShapeDtypeStruct((d, dff), BF16),
                   jax.ShapeDtypeStruct((d, dff), BF16), jax.ShapeDtypeStruct((dff, d), BF16)],
        scratch_shapes=[pltpu.VMEM((r, d), BF16), pltpu.VMEM((r, d), F32)],
        compiler_params=pltpu.CompilerParams(
            dimension_semantics=("arbitrary",), vmem_limit_bytes=_vmem_limit(est)),
        name="out_ffn_cast",
    )(x, a, w_out, g, w_gate, w_up, w_down, g_final)
    return y, (wg, wu, wd)


def _trunk(x, s_hgrn, s_conv, p, ffn_bf16, *, n_seq, t):
    depth = p["norm_mix"].shape[0]
    rows = x.shape[0]
    tm = min(ROW_TILE, rows // 2) if rows >= 2 * MIN_PIPELINED_ROWS else rows
    assert rows % tm == 0
    new_h, new_c = None, []
    for l in range(depth):
        j = l // 2
        if l % 2 == 0:
            proj = _norm_matmul(x, p["norm_mix"], p["hgrn_w_in"], l=l, j=j, tm=tm)
            a, new_h = _gla(proj, s_hgrn, p["hgrn_lb_logits"], p["hgrn_norm"], new_h,
                            layer=j, n_seq=n_seq, t=t)
            w_out = p["hgrn_w_out"]
        else:
            a, s = _conv_mixer(x, p["norm_mix"], p["conv_w_in"], p["conv_w"], s_conv,
                               l=l, j=j, n_seq=n_seq, t=t)
            new_c.append(s)
            w_out = p["conv_w_out"]
        if len(ffn_bf16) <= l:
            x, cast = _out_ffn_cast(x, a, w_out, p["norm_ffn"], p["ffn_w_gate"], p["ffn_w_up"],
                                    p["ffn_w_down"], p["norm_final"], l=l, j=j, final=(l == depth - 1))
            ffn_bf16.append(cast)
        else:
            x = _out_ffn(x, a, w_out, p["norm_ffn"], *ffn_bf16[l], p["norm_final"],
                         l=l, j=j, tm=tm, final=(l == depth - 1))
    return x, new_h, jnp.stack(new_c)


def kernel(x_prompt, x_sample, state_hgrn, state_conv, meta_tokens, norm_mix, norm_ffn, norm_final,
           hgrn_w_in, hgrn_w_out, hgrn_lb_logits, hgrn_norm, conv_w_in, conv_w, conv_w_out,
           ffn_w_gate, ffn_w_up, ffn_w_down):
    bsz, seq, d = x_prompt.shape
    dec_b, dec_t, _ = x_sample.shape
    n_meta = meta_tokens.shape[0]
    p = dict(norm_mix=norm_mix[:, None], norm_ffn=norm_ffn[:, None], norm_final=norm_final[None],
             hgrn_lb_logits=hgrn_lb_logits, hgrn_norm=hgrn_norm[:, None], conv_w=conv_w,
             hgrn_w_in=hgrn_w_in.astype(BF16), hgrn_w_out=hgrn_w_out.astype(BF16),
             conv_w_in=conv_w_in.astype(BF16), conv_w_out=conv_w_out.astype(BF16),
             ffn_w_gate=ffn_w_gate, ffn_w_up=ffn_w_up, ffn_w_down=ffn_w_down)
    ffn_bf16 = []

    zeros_h = jnp.zeros((state_hgrn.shape[0], 1) + state_hgrn.shape[2:], F32)
    zeros_c = jnp.zeros((state_conv.shape[0], 1) + state_conv.shape[2:], F32)
    _, meta_h, meta_c = _trunk(meta_tokens.astype(F32), zeros_h, zeros_c, p, ffn_bf16, n_seq=1, t=n_meta)

    prompt_h0 = jnp.broadcast_to(meta_h, (meta_h.shape[0], bsz) + meta_h.shape[2:])
    prompt_c0 = jnp.broadcast_to(meta_c, (meta_c.shape[0], bsz) + meta_c.shape[2:])
    y_prompt, new_hgrn_prompt, new_conv_prompt = _trunk(
        x_prompt.reshape(bsz * seq, d), prompt_h0, prompt_c0, p, ffn_bf16, n_seq=bsz, t=seq)

    y_sample, new_hgrn_sample, new_conv_sample = _trunk(
        x_sample.reshape(dec_b * dec_t, d), state_hgrn, state_conv, p, ffn_bf16, n_seq=dec_b, t=dec_t)

    return (y_prompt.reshape(bsz, seq, d), y_sample.reshape(dec_b, dec_t, d),
            new_hgrn_prompt, new_conv_prompt, new_hgrn_sample, new_conv_sample)
```

```python
import functools

import jax
import jax.numpy as jnp
import numpy as np
from jax import lax
from jax.experimental import pallas as pl
from jax.experimental.pallas import tpu as pltpu

F32, BF16 = jnp.float32, jnp.bfloat16
EPS = 1e-6
CONV_W = 3

V7X_VMEM_BYTES = 64 * 1024 * 1024
V7X_VMEM_COMPILER_RESERVE = 6 * 1024 * 1024
BF16_SUBLANES = 16
LANES = 128
V7X_MXU_WIDTH = 256
SCAN_LEVELS = (2, 4)

ROW_TILE = 1024
MIN_PIPELINED_ROWS = 256
GLA_CHUNK = 64
GLA_LONG_SEQS_PER_STEP = 8
GLA_SEQS_PER_STEP = 16
GLA_SHORT_SEQS_INTERLEAVED = 4
CONV_ROWS = 1024
CONV_COLS = 256
FF_CHUNKS = 4


def _vmem_limit(est_bytes):
    return int(min(max(est_bytes, 16 * 1024 * 1024), V7X_VMEM_BYTES - V7X_VMEM_COMPILER_RESERVE))


def _rmsnorm(x, g):
    return x * lax.rsqrt(jnp.mean(x * x, axis=-1, keepdims=True) + EPS) * g


def _silu(x):
    half = 0.5 * x
    return half + half * jnp.tanh(half)


def _dot(a, b):
    return jnp.dot(a, b, preferred_element_type=F32)


def _dot_nt(a, b):
    return lax.dot_general(a, b, (((1,), (1,)), ((), ())), preferred_element_type=F32)


def _dot_tn(a, b):
    return lax.dot_general(a, b, (((0,), (0,)), ((), ())), preferred_element_type=F32)


def _resident(shape):
    zeros = (0,) * len(shape)
    return pl.BlockSpec(shape, lambda *_: zeros, pipeline_mode=pl.Buffered(1))


def _layer_of(stacked, l):
    index = (l,) + (0,) * (stacked.ndim - 1)
    return pl.BlockSpec((None,) + stacked.shape[1:], lambda *_: index, pipeline_mode=pl.Buffered(1))


def _norm_matmul_body(x_ref, g_ref, w_ref, o_ref, *, n_chunk):
    h = _rmsnorm(x_ref[...], g_ref[...]).astype(BF16)
    for c in range(0, o_ref.shape[1], n_chunk):
        o_ref[:, c:c + n_chunk] = _dot(h, w_ref[:, c:c + n_chunk])


def _norm_matmul(x, g, w, *, l, j, tm):
    r, d = x.shape
    n = w.shape[2]
    est = 2 * tm * d * 4 + d * n * 2 + 2 * tm * n * 4 + tm * d * 8 + tm * 1024 * 8
    return pl.pallas_call(
        functools.partial(_norm_matmul_body, n_chunk=1024),
        grid=(r // tm,),
        in_specs=[pl.BlockSpec((tm, d), lambda i: (i, 0)), _layer_of(g, l), _layer_of(w, j)],
        out_specs=pl.BlockSpec((tm, n), lambda i: (i, 0)),
        out_shape=jax.ShapeDtypeStruct((r, n), F32),
        compiler_params=pltpu.CompilerParams(
            dimension_semantics=("parallel",), vmem_limit_bytes=_vmem_limit(est)),
        name="hgrn_in_proj",
    )(x, g, w)


def _gla_constants(bt, stack):
    i = np.arange(bt)[:, None]
    t = np.arange(bt)[None, :]
    blocks = [t <= i]
    for s in SCAN_LEVELS:
        mid = (i // (2 * s)) * 2 * s + s - 1
        blocks.append(np.where(i <= mid, (t > i) & (t <= mid), (t > mid) & (t <= i)))
    scan = np.concatenate(blocks, axis=0).astype(np.float32)
    r = np.arange(stack)[:, None]
    c = np.arange(LANES)[None, :] + (r // LANES) * LANES
    x = (r % bt) ^ (c % bt)
    lvl = np.where(x == 0, 0, 1 + np.floor(np.log2(np.maximum(x, 1))).astype(np.int32))
    lvl = np.where((r // bt == c // bt) & (r >= c), lvl, -1).astype(np.int32)
    return jnp.asarray(np.concatenate([scan, scan], axis=1), BF16), jnp.asarray(lvl)


def _gla_body(proj_ref, s0_ref, lbl_ref, gn_ref, scan_ref, lvl_ref, *rest,
              layer, nb, t_rows, bt, n_heads, hk, hv, unrolled):
    og_ref, snew_ref, aux_ref, o_ref, w_ref = rest[-5:]
    st_ref = aux_ref if unrolled else None
    hl_ref = None if unrolled else aux_ref
    n_slots = w_ref.shape[0]
    c = pl.program_id(1)
    key = n_heads * hk
    d = n_heads * hv
    pad = bt - t_rows
    stack = lvl_ref.shape[0]
    group = stack // bt
    width = group * hk
    tiles = stack // LANES

    def proj_rows(i, lo, hi):
        if unrolled:
            return proj_ref[i, :, lo:hi]
        return proj_ref[pl.ds(pl.multiple_of(i * t_rows, t_rows), t_rows), lo:hi]

    if unrolled:
        @pl.when(c == 0)
        def _():
            for i in range(nb):
                for h in range(n_heads):
                    st_ref[i, h] = s0_ref[i, h].T

    logits = lbl_ref[...]
    ex = jnp.exp(logits - jnp.max(logits, axis=0, keepdims=True))
    sm = ex / jnp.sum(ex, axis=0, keepdims=True)
    lb = jnp.sum(sm[0:layer + 1], axis=0, keepdims=True) - sm[0:1]

    om = 1.0 - lb

    sub = lax.broadcasted_iota(jnp.int32, (1, 8, width), 1)
    lvl = lvl_ref[...]

    def padded(x):
        return x if pad == 0 else jnp.concatenate([x, jnp.zeros((pad, x.shape[1]), x.dtype)], axis=0)

    def by_role(s, second_half, first_half):
        shape = second_half.shape
        pick = jnp.where((sub & s) != 0, second_half.reshape(bt // 8, 8, width),
                         first_half.reshape(bt // 8, 8, width))
        return pick.reshape(shape)

    def stacked(x):
        return jnp.concatenate([x[:, e * hk:(e + 1) * hk] for e in range(group)], axis=0).astype(BF16)

    def gram(x, y):
        full = _dot_nt(x, y)
        return jnp.concatenate(
            [full[n * LANES:(n + 1) * LANES, n * LANES:(n + 1) * LANES] for n in range(tiles)], axis=0)

    n_stacks = n_heads // group

    def prepare(i, slot):
        wv = w_ref.at[slot]
        q = padded(proj_rows(i, 0, key))
        z = padded(proj_rows(i, key, 2 * key))
        w = om * jax.nn.sigmoid(z)
        f = lb + w
        lf = jnp.log(f)
        k = om - w
        if pad:
            real = lax.broadcasted_iota(jnp.int32, (bt, key), 0) < t_rows
            f = jnp.where(real, f, 1.0)
            lf = jnp.where(real, lf, 0.0)
            k = jnp.where(real, k, 0.0)
        hi = lf.astype(BF16)
        hl = jnp.concatenate([hi, (lf - hi.astype(F32)).astype(BF16)], axis=0)
        if not unrolled:
            hl_ref[slot] = hl
        wv[0] = _silu(q)
        wv[1] = k
        wv[2] = f
        wv[3:3 + 1 + len(SCAN_LEVELS)] = _dot(scan_ref[...], hl).reshape(1 + len(SCAN_LEVELS), bt, key)

    def intra(i, slot, p):
        wv = w_ref.at[slot]
        lanes = slice(p * width, (p + 1) * width)
        qs, k, f, b = wv[0, :, lanes], wv[1, :, lanes], wv[2, :, lanes], wv[3, :, lanes]
        v = stacked(padded(proj_rows(i, 2 * key + p * width, 2 * key + (p + 1) * width)))

        operands = [by_role(1, qs * f, k)]
        for n, s in enumerate(SCAN_LEVELS):
            operands.append(by_role(s, qs, k) * jnp.exp(wv[4 + n, :, lanes]))
        s = 2 * SCAN_LEVELS[-1]
        while s < bt:
            pieces = []
            for m in range(0, bt, 2 * s):
                mid = b[m + s - 1:m + s]
                pieces.append(k[m:m + s] * jnp.exp(mid - b[m:m + s]))
                pieces.append(qs[m + s:m + 2 * s] * jnp.exp(b[m + s:m + 2 * s] - mid))
            operands.append(jnp.concatenate(pieces, axis=0))
            s *= 2

        a = jnp.where(lvl == 0, gram(stacked(qs), stacked(k)), 0.0)
        for n, x in enumerate(operands):
            x = stacked(x)
            a = jnp.where(lvl == n + 1, gram(x, x), a)
        bl = b[bt - 1:bt]
        return (a.astype(BF16), v, (qs * jnp.exp(b)).astype(BF16), (k * jnp.exp(bl - b)).astype(BF16),
                jnp.exp(bl))

    def finish(i, p, a, v, qb, kt, decay):
        rows = pl.ds(i * t_rows if unrolled else pl.multiple_of(i * t_rows, t_rows), t_rows)
        o_intra = jnp.concatenate(
            [_dot(a[n * LANES:(n + 1) * LANES], v[n * LANES:(n + 1) * LANES]) for n in range(tiles)],
            axis=0)
        for e in range(group):
            h = p * group + e
            head = slice(e * hk, (e + 1) * hk)
            st = st_ref[i, h]
            o = o_intra[e * bt:(e + 1) * bt] + _dot_nt(qb[:, head], st.astype(BF16))
            o_ref[rows, h * hv:(h + 1) * hv] = o[0:t_rows]
            st_ref[i, h] = st * decay[:, head] + _dot_tn(v[e * bt:(e + 1) * bt], kt[:, head])

    def finish_in_place(i, slot, a, v, qb, kt):
        rows = pl.ds(pl.multiple_of(i * t_rows, t_rows), t_rows)
        o_intra = _dot(a, v)
        sum_rows = jnp.concatenate([jnp.ones((2 * bt, hv), BF16), jnp.zeros((2 * bt, hv), BF16)], axis=1)
        for h in range(n_heads):
            head = slice(h * hk, (h + 1) * hk)
            s = s0_ref[i, h]
            o = o_intra[h * bt:(h + 1) * bt] + _dot(qb[:, head], s.astype(BF16))
            o_ref[rows, h * hv:(h + 1) * hv] = o[0:t_rows]
            lhs = jnp.concatenate([hl_ref[slot, :, head], kt[:, head]], axis=0)
            rhs = jnp.concatenate(
                [sum_rows,
                 jnp.concatenate([jnp.zeros((bt, hv), BF16), v[h * bt:(h + 1) * bt]], axis=1)], axis=0)
            upd = _dot_tn(lhs, rhs)
            snew_ref[i, h] = jnp.exp(upd[:, 0:hv]) * s + upd[:, hv:2 * hv]

    def interleaved_seqs(j):
        seqs = [j * n_slots + slot for slot in range(n_slots)]
        for slot, i in enumerate(seqs):
            prepare(i, slot)
        parts = [intra(i, slot, 0)[:4] for slot, i in enumerate(seqs)]
        for slot, i in enumerate(seqs):
            finish_in_place(i, slot, *parts[slot])

    if unrolled:
        prepare(0, 0)
        pending = None
        for i in range(nb):
            for p in range(n_stacks):
                if p == 0 and i + 1 < nb:
                    prepare(i + 1, i + 1)
                current = (i, p) + intra(i, i, p)
                if pending is not None:
                    finish(*pending)
                pending = current
        finish(*pending)
    else:
        lax.fori_loop(0, nb // n_slots, lambda j, _: (interleaved_seqs(j), 0)[1], 0, unroll=True)

    gate_lanes = slice(2 * key + d, 2 * key + 2 * d)
    gate = proj_ref[:, :, gate_lanes].reshape(nb * t_rows, d) if unrolled else proj_ref[:, gate_lanes]
    og = (_rmsnorm(o_ref[...], gn_ref[...]) * _silu(gate)).astype(BF16)
    og_ref[...] = og.reshape(og_ref.shape)

    if unrolled:
        @pl.when(c == pl.num_programs(1) - 1)
        def _():
            for i in range(nb):
                for h in range(n_heads):
                    snew_ref[i, h] = st_ref[i, h].T


def _gla(proj, s0, lb_logits, gn, snew, *, layer, n_seq, t):
    _, _, n_heads, hk, hv = s0.shape
    d = n_heads * hv
    key = n_heads * hk
    unrolled = t >= GLA_CHUNK
    if unrolled:
        nb, t_rows = min(GLA_LONG_SEQS_PER_STEP, n_seq), GLA_CHUNK
    else:
        nb, t_rows = min(GLA_SEQS_PER_STEP, n_seq), t
    bt = max(t_rows, BF16_SUBLANES)
    nc = t // t_rows
    rows = nb * t_rows
    assert t % t_rows == 0 and n_seq % nb == 0 and (unrolled or nc == 1) and rows % BF16_SUBLANES == 0
    stack = min(V7X_MXU_WIDTH, n_heads * bt)
    assert LANES % bt == 0 and stack % LANES == 0 and (n_heads * bt) % stack == 0 and hk == LANES
    assert bt > 2 * SCAN_LEVELS[-1] and (unrolled or stack == n_heads * bt == LANES)
    scan, lvl = _gla_constants(bt, stack)
    n_planes = 4 + len(SCAN_LEVELS)
    n_slots = nb if unrolled else (GLA_SHORT_SEQS_INTERLEAVED if nb % GLA_SHORT_SEQS_INTERLEAVED == 0 else 1)
    state_bytes = nb * n_heads * hk * hv * 4
    est = (2 * rows * proj.shape[1] * 4 + 5 * state_bytes + 4 * rows * d * 4
           + 3 * n_slots * n_planes * bt * key * 4 + 8 * 1024 * 1024)
    body = functools.partial(_gla_body, layer=layer, nb=nb, t_rows=t_rows, bt=bt,
                             n_heads=n_heads, hk=hk, hv=hv, unrolled=unrolled)
    if unrolled:
        proj = proj.reshape(n_seq, t, proj.shape[1])
        row_block = lambda width: pl.BlockSpec((nb, t_rows, width), lambda b, c: (b, c, 0))
        og_shape = (n_seq, t, d)
    else:
        row_block = lambda width: pl.BlockSpec((rows, width), lambda b, c: (b, 0))
        og_shape = (n_seq * t, d)
    state_spec = pl.BlockSpec((None, nb, n_heads, hk, hv), lambda b, c: (layer, b, 0, 0, 0))
    aliased = () if snew is None else (snew,)
    og, snew = pl.pallas_call(
        body,
        grid=(n_seq // nb, nc),
        in_specs=[row_block(proj.shape[-1]),
                  state_spec, _resident(lb_logits.shape), _layer_of(gn, layer),
                  _resident(scan.shape), _resident(lvl.shape)]
                 + [pl.BlockSpec(memory_space=pl.ANY)] * len(aliased),
        out_specs=[row_block(d), state_spec],
        out_shape=[jax.ShapeDtypeStruct(og_shape, BF16), jax.ShapeDtypeStruct(s0.shape, F32)],
        input_output_aliases={6: 1} if aliased else {},
        scratch_shapes=[pltpu.VMEM((nb, n_heads, hv, hk), F32) if unrolled
                        else pltpu.VMEM((n_slots, 2 * bt, key), BF16),
                        pltpu.VMEM((rows, d), F32),
                        pltpu.VMEM((n_slots, n_planes, bt, key), F32)],
        compiler_params=pltpu.CompilerParams(
            dimension_semantics=("parallel", "arbitrary"), vmem_limit_bytes=_vmem_limit(est)),
        name="hgrn_gla",
    )(proj, s0, lb_logits, gn, scan, lvl, *aliased)
    return og.reshape(n_seq * t, d), snew


def _conv_body(x_ref, g_ref, w_ref, wc_ref, buf_ref, a_ref, nbuf_ref, *, nb, tt):
    c = pl.program_id(1)
    d = x_ref.shape[1]
    rows = nb * tt

    @pl.when(c == 0)
    def _():
        nbuf_ref[...] = buf_ref[...]

    h = _rmsnorm(x_ref[...], g_ref[...]).astype(BF16)
    t = lax.broadcasted_iota(jnp.int32, (nb, tt, CONV_COLS), 1)

    def project(n):
        cols = [slice(k * d + n * CONV_COLS, k * d + (n + 1) * CONV_COLS) for k in range(3)]
        return tuple(_dot(h, w_ref[:, s]) for s in cols)

    def tail(n, gate_b, gate_c, third):
        cols = slice(n * CONV_COLS, (n + 1) * CONV_COLS)
        u = (gate_c * third).reshape(nb, tt, CONV_COLS)
        p2, p1 = nbuf_ref[:, 0:1, cols], nbuf_ref[:, 1:2, cols]
        u1 = jnp.where(t == 0, p1, pltpu.roll(u, 1, axis=1))
        u2 = jnp.where(t == 0, p2, jnp.where(t == 1, p1, pltpu.roll(u, 2, axis=1)))
        w0, w1, w2 = (wc_ref[k:k + 1, cols].reshape(1, 1, CONV_COLS) for k in range(3))
        y = w0 * u2 + w1 * u1 + w2 * u
        a_ref[:, cols] = (gate_b * y.reshape(rows, CONV_COLS)).astype(BF16)
        nbuf_ref[:, :, cols] = u[:, tt - 2:tt, :]

    pending = None
    for n in range(d // CONV_COLS):
        current = (n,) + project(n)
        if pending is not None:
            tail(*pending)
        pending = current
    tail(*pending)


def _conv_mixer(x, g, w_in, w_conv, buf, *, l, j, n_seq, t):
    r, d = x.shape
    if t >= CONV_ROWS:
        nb, tt = 1, CONV_ROWS
    else:
        nb, tt = min(CONV_ROWS // t, n_seq), t
        if nb == n_seq and nb * t >= 2 * MIN_PIPELINED_ROWS and nb % 2 == 0:
            nb //= 2
    nc = t // tt
    rows = nb * tt
    assert t % tt == 0 and n_seq % nb == 0 and (nb == 1 or nc == 1) and tt % 8 == 0 and tt >= CONV_W - 1
    est = 2 * rows * d * 4 + d * 3 * d * 2 + 2 * rows * d * 2 + 10 * rows * d * 4
    buf_spec = pl.BlockSpec((None, nb, CONV_W - 1, d), lambda b, c: (j, b, 0, 0))
    nbuf_spec = pl.BlockSpec((nb, CONV_W - 1, d), lambda b, c: (b, 0, 0))
    return pl.pallas_call(
        functools.partial(_conv_body, nb=nb, tt=tt),
        grid=(n_seq // nb, nc),
        in_specs=[pl.BlockSpec((rows, d), lambda b, c: (b * nc + c, 0)), _layer_of(g, l),
                  _layer_of(w_in, j), _layer_of(w_conv, j), buf_spec],
        out_specs=[pl.BlockSpec((rows, d), lambda b, c: (b * nc + c, 0)), nbuf_spec],
        out_shape=[jax.ShapeDtypeStruct((r, d), BF16), jax.ShapeDtypeStruct(buf.shape[1:], F32)],
        compiler_params=pltpu.CompilerParams(
            dimension_semantics=("parallel", "arbitrary"), vmem_limit_bytes=_vmem_limit(est)),
        name="conv_mixer",
    )(x, g, w_in, w_conv, buf)


def _ff_chunks(dff):
    tiles = dff // V7X_MXU_WIDTH
    assert tiles * V7X_MXU_WIDTH == dff
    bounds = [round(n * tiles / FF_CHUNKS) * V7X_MXU_WIDTH for n in range(FF_CHUNKS + 1)]
    return tuple(zip(bounds[:-1], bounds[1:]))


def _out_ffn_body(x_ref, a_ref, wo_ref, g_ref, wg_ref, wu_ref, wd_ref, gf_ref, y_ref, *,
                  ff_chunks, final):
    x = x_ref[...] + _dot(a_ref[...], wo_ref[...])
    h = _rmsnorm(x, g_ref[...]).astype(BF16)
    for lo, hi in ff_chunks:
        act = _silu(_dot(h, wg_ref[:, lo:hi])) * _dot(h, wu_ref[:, lo:hi])
        x = x + _dot(act.astype(BF16), wd_ref[lo:hi, :])
    y_ref[...] = _rmsnorm(x, gf_ref[...]) if final else x


def _out_ffn(x, a, w_out, g, w_gate, w_up, w_down, g_final, *, l, j, tm, final):
    r, d = x.shape
    dff = w_gate.shape[1]
    ff_chunks = _ff_chunks(dff)
    widest = max(hi - lo for lo, hi in ff_chunks)
    weights = (d * d + 3 * d * dff) * 2
    est = 2 * tm * d * (4 + 2 + 4) + weights + 3 * tm * d * 4 + tm * widest * 14
    row = lambda i: (i, 0)
    return pl.pallas_call(
        functools.partial(_out_ffn_body, ff_chunks=ff_chunks, final=final),
        grid=(r // tm,),
        in_specs=[pl.BlockSpec((tm, d), row), pl.BlockSpec((tm, d), row), _layer_of(w_out, j),
                  _layer_of(g, l), _resident(w_gate.shape), _resident(w_up.shape), _resident(w_down.shape),
                  _resident(g_final.shape)],
        out_specs=pl.BlockSpec((tm, d), row),
        out_shape=jax.ShapeDtypeStruct((r, d), F32),
        compiler_params=pltpu.CompilerParams(
            dimension_semantics=("parallel",), vmem_limit_bytes=_vmem_limit(est)),
        name="out_ffn",
    )(x, a, w_out, g, w_gate, w_up, w_down, g_final)


def _out_ffn_cast_body(x_ref, a_ref, wo_ref, g_ref, wg_ref, wu_ref, wd_ref, gf_ref,
                       y_ref, wgb_ref, wub_ref, wdb_ref, h_ref, acc_ref, *, final):
    c = pl.program_id(0)

    @pl.when(c == 0)
    def _():
        x = x_ref[...] + _dot(a_ref[...], wo_ref[...])
        h_ref[...] = _rmsnorm(x, g_ref[...]).astype(BF16)
        acc_ref[...] = x

    wg, wu, wd = wg_ref[...].astype(BF16), wu_ref[...].astype(BF16), wd_ref[...].astype(BF16)
    wgb_ref[...], wub_ref[...], wdb_ref[...] = wg, wu, wd
    h = h_ref[...]
    acc_ref[...] += _dot((_silu(_dot(h, wg)) * _dot(h, wu)).astype(BF16), wd)

    @pl.when(c == pl.num_programs(0) - 1)
    def _():
        y_ref[...] = _rmsnorm(acc_ref[...], gf_ref[...]) if final else acc_ref[...]


def _out_ffn_cast(x, a, w_out, g, w_gate, w_up, w_down, g_final, *, l, j, final):
    r, d = x.shape
    dff = w_gate.shape[2]
    fc = V7X_MXU_WIDTH
    assert dff % fc == 0
    est = 4 * r * d * 4 + d * d * 2 + 2 * 3 * d * fc * (4 + 2) + 3 * d * fc * 2 + 8 * r * fc * 4
    whole = lambda shape: pl.BlockSpec(shape, lambda c: (0,) * len(shape))
    y, wg, wu, wd = pl.pallas_call(
        functools.partial(_out_ffn_cast_body, final=final),
        grid=(dff // fc,),
        in_specs=[whole((r, d)), whole((r, d)), _layer_of(w_out, j), _layer_of(g, l),
                  pl.BlockSpec((None, d, fc), lambda c: (l, 0, c)),
                  pl.BlockSpec((None, d, fc), lambda c: (l, 0, c)),
                  pl.BlockSpec((None, fc, d), lambda c: (l, c, 0)),
                  _resident(g_final.shape)],
        out_specs=[whole((r, d)), pl.BlockSpec((d, fc), lambda c: (0, c)),
                   pl.BlockSpec((d, fc), lambda c: (0, c)), pl.BlockSpec((fc, d), lambda c: (c, 0))],
        out_shape=[jax.ShapeDtypeStruct((r, d), F32), jax.ShapeDtypeStruct((d, dff), BF16),
                   jax.ShapeDtypeStruct((d, dff), BF16), jax.ShapeDtypeStruct((dff, d), BF16)],
        scratch_shapes=[pltpu.VMEM((r, d), BF16), pltpu.VMEM((r, d), F32)],
        compiler_params=pltpu.CompilerParams(
            dimension_semantics=("arbitrary",), vmem_limit_bytes=_vmem_limit(est)),
        name="out_ffn_cast",
    )(x, a, w_out, g, w_gate, w_up, w_down, g_final)
    return y, (wg, wu, wd)


def _trunk(x, s_hgrn, s_conv, p, ffn_bf16, *, n_seq, t):
    depth = p["norm_mix"].shape[0]
    rows = x.shape[0]
    tm = min(ROW_TILE, rows // 2) if rows >= 2 * MIN_PIPELINED_ROWS else rows
    assert rows % tm == 0
    new_h, new_c = None, []
    for l in range(depth):
        j = l // 2
        if l % 2 == 0:
            proj = _norm_matmul(x, p["norm_mix"], p["hgrn_w_in"], l=l, j=j, tm=tm)
            a, new_h = _gla(proj, s_hgrn, p["hgrn_lb_logits"], p["hgrn_norm"], new_h,
                            layer=j, n_seq=n_seq, t=t)
            w_out = p["hgrn_w_out"]
        else:
            a, s = _conv_mixer(x, p["norm_mix"], p["conv_w_in"], p["conv_w"], s_conv,
                               l=l, j=j, n_seq=n_seq, t=t)
            new_c.append(s)
            w_out = p["conv_w_out"]
        if len(ffn_bf16) <= l:
            x, cast = _out_ffn_cast(x, a, w_out, p["norm_ffn"], p["ffn_w_gate"], p["ffn_w_up"],
                                    p["ffn_w_down"], p["norm_final"], l=l, j=j, final=(l == depth - 1))
            ffn_bf16.append(cast)
        else:
            x = _out_ffn(x, a, w_out, p["norm_ffn"], *ffn_bf16[l], p["norm_final"],
                         l=l, j=j, tm=tm, final=(l == depth - 1))
    return x, new_h, jnp.stack(new_c)


def kernel(x_prompt, x_sample, state_hgrn, state_conv, meta_tokens, norm_mix, norm_ffn, norm_final,
           hgrn_w_in, hgrn_w_out, hgrn_lb_logits, hgrn_norm, conv_w_in, conv_w, conv_w_out,
           ffn_w_gate, ffn_w_up, ffn_w_down):
    bsz, seq, d = x_prompt.shape
    dec_b, dec_t, _ = x_sample.shape
    n_meta = meta_tokens.shape[0]
    p = dict(norm_mix=norm_mix[:, None], norm_ffn=norm_ffn[:, None], norm_final=norm_final[None],
             hgrn_lb_logits=hgrn_lb_logits, hgrn_norm=hgrn_norm[:, None], conv_w=conv_w,
             hgrn_w_in=hgrn_w_in.astype(BF16), hgrn_w_out=hgrn_w_out.astype(BF16),
             conv_w_in=conv_w_in.astype(BF16), conv_w_out=conv_w_out.astype(BF16),
             ffn_w_gate=ffn_w_gate, ffn_w_up=ffn_w_up, ffn_w_down=ffn_w_down)
    ffn_bf16 = []

    zeros_h = jnp.zeros((state_hgrn.shape[0], 1) + state_hgrn.shape[2:], F32)
    zeros_c = jnp.zeros((state_conv.shape[0], 1) + state_conv.shape[2:], F32)
    _, meta_h, meta_c = _trunk(meta_tokens.astype(F32), zeros_h, zeros_c, p, ffn_bf16, n_seq=1, t=n_meta)

    prompt_h0 = jnp.broadcast_to(meta_h, (meta_h.shape[0], bsz) + meta_h.shape[2:])
    prompt_c0 = jnp.broadcast_to(meta_c, (meta_c.shape[0], bsz) + meta_c.shape[2:])
    y_prompt, new_hgrn_prompt, new_conv_prompt = _trunk(
        x_prompt.reshape(bsz * seq, d), prompt_h0, prompt_c0, p, ffn_bf16, n_seq=bsz, t=seq)

    y_sample, new_hgrn_sample, new_conv_sample = _trunk(
        x_sample.reshape(dec_b * dec_t, d), state_hgrn, state_conv, p, ffn_bf16, n_seq=dec_b, t=dec_t)

    return (y_prompt.reshape(bsz, seq, d), y_sample.reshape(dec_b, dec_t, d),
            new_hgrn_prompt, new_conv_prompt, new_hgrn_sample, new_conv_sample)
```
